```python
import jax
import jax.numpy as jnp
from jax import lax
import numpy as np

D_MODEL = 1024
BATCH = 1
SEQ = 16384
DEPTH = 1
DEC_BATCH = 128
DEC_SEQ = 4
PAST_LEN = 16384
PAGE_SIZE = 128

MLA_HEADS = 16
QK_NOPE = 64
QK_ROPE = 32
QK_DIM = QK_NOPE + QK_ROPE
V_HEAD = 64
Q_LORA = 384
KV_LORA = 256
ROPE_THETA = 10000.0
SM_SCALE = QK_DIM ** -0.5
Q_BLOCK = 128
C_CONV = D_MODEL
CONV_W = 31
N_MEM = 256
MEM_HEADS = 4
MEM_HEAD_DIM = D_MODEL // MEM_HEADS
D_FF = ((8 * D_MODEL // 3 + 127) // 128) * 128
FFN_CONV_W = 3
N_BRANCH = 3
EPS = 1e-6
SPLIT_QA = 2 * C_CONV
SPLIT_KVA = SPLIT_QA + Q_LORA
SPLIT_MQ = SPLIT_KVA + KV_LORA + QK_ROPE
SPLIT_GATE = SPLIT_MQ + MEM_HEADS * MEM_HEAD_DIM
IN_COLS = SPLIT_GATE + N_BRANCH * D_MODEL

kernel_name = 'hybrid_conformer_mla_memory_decoder_step'


def _rmsnorm(x, g):
    xf = x.astype(jnp.float32)
    y = xf * lax.rsqrt(jnp.mean(xf * xf, axis=-1, keepdims=True) + EPS)
    return y.astype(x.dtype) * g


def _layernorm(x, g, b):
    xf = x.astype(jnp.float32)
    mu = jnp.mean(xf, axis=-1, keepdims=True)
    var = jnp.mean(jnp.square(xf - mu), axis=-1, keepdims=True)
    return ((xf - mu) * lax.rsqrt(var + EPS)).astype(x.dtype) * g + b


def _rope(x, pos):
    half = QK_ROPE // 2
    inv_freq = ROPE_THETA ** (-jnp.arange(half, dtype=jnp.float32) / half)
    ang = pos.astype(jnp.float32)[:, None] * inv_freq[None, :]
    cos, sin = jnp.cos(ang), jnp.sin(ang)
    if x.ndim == 4:
        cos, sin = cos[:, None, :], sin[:, None, :]
    xf = x.astype(jnp.float32)
    x1, x2 = xf[..., :half], xf[..., half:]
    return jnp.concatenate([x1 * cos - x2 * sin, x2 * cos + x1 * sin], axis=-1).astype(x.dtype)


def _causal_dwconv(x_ext, w, b):
    ch = x_ext.shape[-1]
    y = lax.conv_general_dilated(x_ext, w[:, None, :].astype(x_ext.dtype), window_strides=(1,), padding='VALID',
                                 dimension_numbers=('NWC', 'WIO', 'NWC'), feature_group_count=ch)
    return y + b


def _front_end(x, pos, norm1_g, w_in, q_a_norm_g, w_uq, kv_a_norm_g, w_uk, q_norm_g, k_norm_g, mq_norm_g):
    b, s, _ = x.shape
    z = _rmsnorm(x, norm1_g) @ w_in
    glu_in, q_a, kv_a, mq, gate_logits = jnp.split(z, [SPLIT_QA, SPLIT_KVA, SPLIT_MQ, SPLIT_GATE], axis=-1)
    u = glu_in[..., :C_CONV] * jax.nn.sigmoid(glu_in[..., C_CONV:])
    q = (_rmsnorm(q_a, q_a_norm_g) @ w_uq).reshape(b, s, MLA_HEADS, QK_DIM)
    q = _rmsnorm(q, q_norm_g)
    q_nope, q_pe = q[..., :QK_NOPE], _rope(q[..., QK_NOPE:], pos)
    c_kv = _rmsnorm(kv_a[..., :KV_LORA], kv_a_norm_g)
    k_pe_raw = kv_a[..., KV_LORA:]
    k_nope = jnp.einsum('bsc,chd->bshd', c_kv, w_uk)
    kn = k_nope.astype(jnp.float32)
    kp = k_pe_raw.astype(jnp.float32)
    ms = (jnp.sum(kn * kn, axis=-1) + jnp.sum(kp * kp, axis=-1)[..., None]) / QK_DIM
    k_scale = lax.rsqrt(ms + EPS).astype(x.dtype)
    k_pe = _rope(k_pe_raw * k_norm_g[QK_NOPE:], pos)
    mq = _rmsnorm(mq.reshape(b, s, MEM_HEADS, MEM_HEAD_DIM), mq_norm_g)
    return u, q_nope, q_pe, c_kv, k_nope, k_pe, k_scale, mq, gate_logits


def _conv_branch(u, u_prev, conv_w, conv_b, ln_g, ln_b, w_conv_out):
    ext = jnp.concatenate([u_prev, u], axis=1)
    hc = _causal_dwconv(ext, conv_w, conv_b)
    a = jax.nn.silu(_layernorm(hc, ln_g, ln_b)) @ w_conv_out
    return a, ext[:, ext.shape[1] - (CONV_W - 1):]


def _mla_prompt(q_nope, q_pe, c_kv, k_nope, k_pe, k_scale, g_k_nope, w_uv):
    b, s, h, _ = q_nope.shape
    q = jnp.concatenate([q_nope, q_pe], axis=-1)
    k = jnp.concatenate([k_nope * g_k_nope, jnp.broadcast_to(k_pe[:, :, None, :], (b, s, h, QK_ROPE))], axis=-1)
    v = jnp.einsum('bsc,chd->bshd', c_kv, w_uv)
    ksc = jnp.transpose(k_scale, (0, 2, 1))[:, :, None, :].astype(jnp.float32)
    n_blk = s // Q_BLOCK
    q_blocks = jnp.moveaxis(q.reshape(b, n_blk, Q_BLOCK, h, QK_DIM), 1, 0)
    k_pos = jnp.arange(s)

    def attend_block(args):
        q_blk, i = args
        sc = jnp.einsum('bqhd,bkhd->bhqk', q_blk, k, preferred_element_type=jnp.float32) * ksc * SM_SCALE
        q_pos = i * Q_BLOCK + jnp.arange(Q_BLOCK)
        sc = jnp.where(k_pos[None, :] <= q_pos[:, None], sc, -jnp.inf)
        p = jax.nn.softmax(sc, axis=-1).astype(v.dtype)
        return jnp.einsum('bhqk,bkhd->bqhd', p, v)

    o = lax.map(attend_block, (q_blocks, jnp.arange(n_blk)))
    return jnp.moveaxis(o, 0, 1).reshape(b, s, h, V_HEAD)


def _mla_sample(q_nope, q_pe, c_new, kpe_new, ksc_new, ckv_pool, kpe_pool, ksc_pool, layer, page_table,
                g_k_nope, w_uk, w_uv):
    t_new = c_new.shape[1]
    n_past = page_table.shape[1] * PAGE_SIZE
    q_lat = jnp.einsum('bshd,chd->bshc', q_nope * g_k_nope, w_uk)
    t_pos = jnp.arange(n_past + t_new)
    visible = (t_pos[None, :] < n_past) | (t_pos[None, :] - n_past <= jnp.arange(t_new)[:, None])

    def attend_seq(args):
        ql, qp, pages, cn, kn, sn = args
        c = jnp.concatenate([ckv_pool[layer, pages].reshape(n_past, KV_LORA), cn], axis=0)
        kp = jnp.concatenate([kpe_pool[layer, pages].reshape(n_past, QK_ROPE), kn], axis=0)
        ks = jnp.concatenate([ksc_pool[layer, pages].reshape(n_past, MLA_HEADS), sn], axis=0)
        sc = (jnp.einsum('shc,tc->hst', ql, c, preferred_element_type=jnp.float32)
              + jnp.einsum('shr,tr->hst', qp, kp, preferred_element_type=jnp.float32))
        sc = sc * ks.T.astype(jnp.float32)[:, None, :] * SM_SCALE
        sc = jnp.where(visible[None], sc, -jnp.inf)
        p = jax.nn.softmax(sc, axis=-1).astype(c.dtype)
        o_lat = jnp.einsum('hst,tc->shc', p, c)
        return jnp.einsum('shc,chd->shd', o_lat, w_uv)

    return lax.map(attend_seq, (q_lat, q_pe, page_table, c_new, kpe_new, ksc_new))


def _mem_kv(mem, mem_norm_g, w_mk, w_mv, mk_norm_g):
    m = _rmsnorm(mem, mem_norm_g)
    b, n, _ = m.shape
    k = _rmsnorm((m @ w_mk).reshape(b, n, MEM_HEADS, MEM_HEAD_DIM), mk_norm_g)
    v = (m @ w_mv).reshape(b, n, MEM_HEADS, MEM_HEAD_DIM)
    return k, v


def _mem_attend(mq, mk, mv):
    sc = jnp.einsum('bshd,bnhd->bhsn', mq, mk, preferred_element_type=jnp.float32) * (MEM_HEAD_DIM ** -0.5)
    p = jax.nn.softmax(sc, axis=-1).astype(mv.dtype)
    return jnp.einsum('bhsn,bnhd->bshd', p, mv)


def _merge(x, a, mla_o, mem_o, gate_logits, w_o_mla, w_mo, w_out):
    b, s, _ = x.shape
    br_b = mla_o.reshape(b, s, MLA_HEADS * V_HEAD) @ w_o_mla
    br_c = mem_o.reshape(b, s, MEM_HEADS * MEM_HEAD_DIM) @ w_mo
    g = jax.nn.sigmoid(gate_logits).reshape(b, s, N_BRANCH, D_MODEL)
    return x + (g[:, :, 0] * a + g[:, :, 1] * br_b + g[:, :, 2] * br_c) @ w_out


def _ffn(x, up_prev, norm2_g, w_up, ffn_conv_w, ffn_conv_b, w_down):
    up = _rmsnorm(x, norm2_g) @ w_up
    ext = jnp.concatenate([up_prev, up], axis=1)
    hc = _causal_dwconv(ext, ffn_conv_w, ffn_conv_b)
    y = x + (jax.nn.silu(hc[..., :D_FF]) * hc[..., D_FF:]) @ w_down
    return y, ext[:, ext.shape[1] - (FFN_CONV_W - 1):]


def setup_inputs(seed: int = 0) -> dict:
    key = jax.random.key(seed)
    ks = iter(jax.random.split(key, 48))
    f32 = jnp.float32

    def nrm(shape, scale=1.0):
        return scale * jax.random.normal(next(ks), shape, f32)

    def gain(n):
        return 1.0 + nrm((DEPTH, n), 0.02)

    n_pages = PAST_LEN // PAGE_SIZE
    n_used = DEC_BATCH * n_pages
    n_pool = n_used + max(1, n_used // 4)
    page_table = jax.random.permutation(next(ks), n_pool)[:n_used].reshape(DEC_BATCH, n_pages).astype(jnp.int32)
    return {
        'x_prompt': nrm((BATCH, SEQ, D_MODEL)),
        'x_sample': nrm((DEC_BATCH, DEC_SEQ, D_MODEL)),
        'mem_prompt': nrm((BATCH, N_MEM, D_MODEL)),
        'cache_ckv': nrm((DEPTH, n_pool, PAGE_SIZE, KV_LORA)),
        'cache_kpe': nrm((DEPTH, n_pool, PAGE_SIZE, QK_ROPE)),
        'cache_kscale': jnp.exp(nrm((DEPTH, n_pool, PAGE_SIZE, MLA_HEADS), 0.1)),
        'page_table': page_table,
        'state_conv': nrm((DEPTH, DEC_BATCH, CONV_W - 1, C_CONV), 0.5),
        'state_ffn': nrm((DEPTH, DEC_BATCH, FFN_CONV_W - 1, 2 * D_FF)),
        'cache_mem_k': nrm((DEPTH, DEC_BATCH, N_MEM, MEM_HEADS, MEM_HEAD_DIM)),
        'cache_mem_v': nrm((DEPTH, DEC_BATCH, N_MEM, MEM_HEADS, MEM_HEAD_DIM)),
        'norm1_g': gain(D_MODEL),
        'w_in': nrm((DEPTH, D_MODEL, IN_COLS), D_MODEL ** -0.5),
        'q_a_norm_g': gain(Q_LORA),
        'w_uq': nrm((DEPTH, Q_LORA, MLA_HEADS * QK_DIM), Q_LORA ** -0.5),
        'kv_a_norm_g': gain(KV_LORA),
        'w_uk': nrm((DEPTH, KV_LORA, MLA_HEADS, QK_NOPE), KV_LORA ** -0.5),
        'w_uv': nrm((DEPTH, KV_LORA, MLA_HEADS, V_HEAD), KV_LORA ** -0.5),
        'q_norm_g': gain(QK_DIM),
        'k_norm_g': gain(QK_DIM),
        'w_o_mla': nrm((DEPTH, MLA_HEADS * V_HEAD, D_MODEL), (MLA_HEADS * V_HEAD) ** -0.5),
        'conv_w': nrm((DEPTH, CONV_W, C_CONV), CONV_W ** -0.5),
        'conv_b': nrm((DEPTH, C_CONV), 0.01),
        'conv_ln_g': gain(C_CONV),
        'conv_ln_b': nrm((DEPTH, C_CONV), 0.01),
        'w_conv_out': nrm((DEPTH, C_CONV, D_MODEL), C_CONV ** -0.5),
        'mem_norm_g': gain(D_MODEL),
        'w_mk': nrm((DEPTH, D_MODEL, MEM_HEADS * MEM_HEAD_DIM), D_MODEL ** -0.5),
        'w_mv': nrm((DEPTH, D_MODEL, MEM_HEADS * MEM_HEAD_DIM), D_MODEL ** -0.5),
        'mq_norm_g': gain(MEM_HEAD_DIM),
        'mk_norm_g': gain(MEM_HEAD_DIM),
        'w_mo': nrm((DEPTH, MEM_HEADS * MEM_HEAD_DIM, D_MODEL), (MEM_HEADS * MEM_HEAD_DIM) ** -0.5),
        'w_out': nrm((DEPTH, D_MODEL, D_MODEL), D_MODEL ** -0.5),
        'norm2_g': gain(D_MODEL),
        'w_up': nrm((DEPTH, D_MODEL, 2 * D_FF), D_MODEL ** -0.5),
        'ffn_conv_w': nrm((DEPTH, FFN_CONV_W, 2 * D_FF), FFN_CONV_W ** -0.5),
        'ffn_conv_b': nrm((DEPTH, 2 * D_FF), 0.01),
        'w_down': nrm((DEPTH, D_FF, D_MODEL), D_FF ** -0.5),
    }


def reference(x_prompt, x_sample, mem_prompt, cache_ckv, cache_kpe, cache_kscale, page_table, state_conv, state_ffn,
              cache_mem_k, cache_mem_v, norm1_g, w_in, q_a_norm_g, w_uq, kv_a_norm_g, w_uk, w_uv, q_norm_g, k_norm_g,
              w_o_mla, conv_w, conv_b, conv_ln_g, conv_ln_b, w_conv_out, mem_norm_g, w_mk, w_mv, mq_norm_g, mk_norm_g,
              w_mo, w_out, norm2_g, w_up, ffn_conv_w, ffn_conv_b, w_down):
    n_past = page_table.shape[1] * PAGE_SIZE
    b_p, s_p, _ = x_prompt.shape
    pos_p = jnp.arange(s_p)
    pos_s = n_past + jnp.arange(x_sample.shape[1])
    xp, xs = x_prompt, x_sample
    layer_states = []
    for l in range(DEPTH):
        fe_w = (norm1_g[l], w_in[l], q_a_norm_g[l], w_uq[l], kv_a_norm_g[l], w_uk[l], q_norm_g[l], k_norm_g[l],
                mq_norm_g[l])
        conv_wts = (conv_w[l], conv_b[l], conv_ln_g[l], conv_ln_b[l], w_conv_out[l])
        merge_w = (w_o_mla[l], w_mo[l], w_out[l])
        ffn_w = (norm2_g[l], w_up[l], ffn_conv_w[l], ffn_conv_b[l], w_down[l])
        g_k_nope = k_norm_g[l][:QK_NOPE]
        u, qn, qpe, c_p, kn, kpe_p, ksc_p, mq, gl = _front_end(xp, pos_p, *fe_w)
        a, conv_p = _conv_branch(u, jnp.zeros((b_p, CONV_W - 1, C_CONV), u.dtype), *conv_wts)
        mla_o = _mla_prompt(qn, qpe, c_p, kn, kpe_p, ksc_p, g_k_nope, w_uv[l])
        mk_p, mv_p = _mem_kv(mem_prompt, mem_norm_g[l], w_mk[l], w_mv[l], mk_norm_g[l])
        h = _merge(xp, a, mla_o, _mem_attend(mq, mk_p, mv_p), gl, *merge_w)
        xp, ffn_p = _ffn(h, jnp.zeros((b_p, FFN_CONV_W - 1, 2 * D_FF), h.dtype), *ffn_w)
        u, qn, qpe, c_s, _, kpe_s, ksc_s, mq, gl = _front_end(xs, pos_s, *fe_w)
        a, conv_s = _conv_branch(u, state_conv[l], *conv_wts)
        mla_o = _mla_sample(qn, qpe, c_s, kpe_s, ksc_s, cache_ckv, cache_kpe, cache_kscale, l, page_table,
                            g_k_nope, w_uk[l], w_uv[l])
        h = _merge(xs, a, mla_o, _mem_attend(mq, cache_mem_k[l], cache_mem_v[l]), gl, *merge_w)
        xs, ffn_s = _ffn(h, state_ffn[l], *ffn_w)
        layer_states.append((c_p, kpe_p, ksc_p, conv_p, ffn_p, mk_p, mv_p, c_s, kpe_s, ksc_s, conv_s, ffn_s))
    (p_ckv, p_kpe, p_kscale, p_conv, p_ffn, p_mem_k, p_mem_v,
     s_ckv, s_kpe, s_kscale, s_conv, s_ffn) = [jnp.stack(st) for st in zip(*layer_states)]
    return (xp, xs, p_ckv, p_kpe, p_kscale, p_conv, p_ffn, p_mem_k, p_mem_v, s_ckv, s_kpe, s_kscale, s_conv, s_ffn)
```

```python
import functools

import jax
import jax.numpy as jnp
from jax import lax
from jax.experimental import pallas as pl
from jax.experimental.pallas import tpu as pltpu

F32 = jnp.float32
BF16 = jnp.bfloat16
EPS = 1e-6
ROPE_THETA = 10000.0
HEAD_PAD = 128
VMEM_LIMIT = 56 * 1024 * 1024
ROW_TILE = 256
ATTN_TILE = 512
PAGES_PER_CHUNK = 16
NEG_INF = float("-inf")


def _const_spec(shape):
    nd = len(shape)
    return pl.BlockSpec(shape, lambda *_: (0,) * nd)


def _row_spec(tm, width):
    return pl.BlockSpec((tm, width), lambda i: (i, 0))


def _params(*sem):
    return pltpu.CompilerParams(dimension_semantics=sem, vmem_limit_bytes=VMEM_LIMIT)


def _bdot(a, b):
    return jnp.dot(a.astype(BF16), b.astype(BF16), preferred_element_type=F32)


def _bdot_nt(a, b):
    return lax.dot_general(a.astype(BF16), b.astype(BF16), (((1,), (1,)), ((), ())), preferred_element_type=F32)


def _rms(x, g):
    return x * lax.rsqrt(jnp.mean(x * x, axis=-1, keepdims=True) + EPS) * g


def _front_end_kernel(dims, sample, x_ref, cos_ref, sin_ref, g1_ref, wmain_ref, gqa_ref, wuq_ref, wuqr_ref,
                      gq_ref, gqr_ref, gkv_ref, wuk_ref, gkn_ref, gkp_ref, gkpr_ref, gmq_ref, *rest):
    c_conv, q_lora, kv_lora, n_heads, nope, rope, mem_heads, mem_dim = dims
    qk_dim = nope + rope
    sm_scale = qk_dim ** -0.5
    if sample:
        wabs_ref, sel_ref, u_ref, ckv_ref, kpe_ref, ksc_ref, mq_ref, qlat_ref, qpe_ref, qs_ref = rest
    else:
        wuv_ref, u_ref, ckv_ref, kpe_ref, ksc_ref, mq_ref, q_ref, k_ref, v_ref = rest
    hp = HEAD_PAD
    h = _rms(x_ref[...], g1_ref[...]).astype(BF16)
    z = jnp.dot(h, wmain_ref[...], preferred_element_type=F32)
    o_qa = 2 * c_conv
    o_c = o_qa + q_lora
    o_pe = o_c + kv_lora
    o_per = o_pe + hp
    o_mq = o_per + hp
    u_ref[...] = z[:, :c_conv] * jax.nn.sigmoid(z[:, c_conv:o_qa])

    cos_t = cos_ref[...]
    sin_t = sin_ref[...]
    qa = _rms(z[:, o_qa:o_c], gqa_ref[...]).astype(BF16)
    q = jnp.dot(qa, wuq_ref[...], preferred_element_type=F32)
    qr = jnp.dot(qa, wuqr_ref[...], preferred_element_type=F32)
    g_cos = gq_ref[...] * cos_t
    g_sin = gqr_ref[...] * sin_t
    q_out = qs_ref if sample else q_ref
    for hd in range(n_heads):
        sl = slice(hd * hp, (hd + 1) * hp)
        qh = q[:, sl]
        r = lax.rsqrt(jnp.sum(qh * qh, axis=-1, keepdims=True) / qk_dim + EPS)
        qn = r * (qh * g_cos + qr[:, sl] * g_sin) * sm_scale
        q_out[:, sl] = qn.astype(BF16)
        if sample:
            qlat_ref[:, hd * kv_lora:(hd + 1) * kv_lora] = _bdot(qn * gkn_ref[...], wabs_ref[hd]).astype(BF16)
    if sample:
        qpe_ref[...] = jnp.dot(qs_ref[...], sel_ref[...], preferred_element_type=F32).astype(BF16)

    c_kv = _rms(z[:, o_c:o_pe], gkv_ref[...])
    ckv_ref[...] = c_kv
    pe = z[:, o_pe:o_per]
    k_pe = pe * (gkp_ref[...] * cos_t) + z[:, o_per:o_mq] * (gkpr_ref[...] * sin_t)
    kpe_ref[...] = k_pe
    ss_pe = jnp.sum(pe * pe, axis=-1, keepdims=True)
    c_bf = c_kv.astype(BF16)
    k_nope = jnp.dot(c_bf, wuk_ref[...], preferred_element_type=F32)
    lane = lax.broadcasted_iota(jnp.int32, (x_ref.shape[0], hp), 1)
    ksc_all = jnp.zeros((x_ref.shape[0], hp), F32)
    for hd in range(n_heads):
        sl = slice(hd * hp, (hd + 1) * hp)
        kn = k_nope[:, sl]
        ksc = lax.rsqrt((jnp.sum(kn * kn, axis=-1, keepdims=True) + ss_pe) / qk_dim + EPS)
        ksc_all = jnp.where(lane == hd, ksc, ksc_all)
        if not sample:
            k_ref[:, sl] = ((kn * gkn_ref[...] + k_pe) * ksc).astype(BF16)
    ksc_ref[...] = ksc_all
    if not sample:
        v_ref[...] = jnp.dot(c_bf, wuv_ref[...], preferred_element_type=F32).astype(BF16)

    for hd in range(mem_heads):
        sl = slice(hd * mem_dim, (hd + 1) * mem_dim)
        mq_ref[:, sl] = (_rms(z[:, o_mq + hd * mem_dim:o_mq + (hd + 1) * mem_dim], gmq_ref[...])
                         * (mem_dim ** -0.5)).astype(BF16)


def _front_end(x, cos_t, sin_t, w, dims, sample, tm):
    n, d = x.shape
    c_conv, q_lora, kv_lora, n_heads, nope, rope, mem_heads, mem_dim = dims
    hp = HEAD_PAD
    ins =[x, cos_t, sin_t, w["g1"], w["w_main"], w["gqa"], w["w_uq_pad"], w["w_uq_rot"], w["gq_pad"], w["gq_rot"],
           w["gkv"], w["w_uk_pad"], w["gkn_pad"], w["gkp_pad"], w["gkp_rot"], w["gmq"]]
    ins += [w["w_abs"], w["sel_pe"]] if sample else [w["w_uv_pad"]]
    in_specs = [_row_spec(tm, d), _row_spec(tm, hp), _row_spec(tm, hp)] + [_const_spec(a.shape) for a in ins[3:]]
    outs = [(c_conv, F32), (kv_lora, F32), (hp, F32), (hp, F32), (mem_heads * mem_dim, BF16)]
    if sample:
        outs += [(n_heads * kv_lora, BF16), (n_heads * rope, BF16), (n_heads * hp, BF16)]
    else:
        outs += [(n_heads * hp, BF16)] * 3
    return pl.pallas_call(
        functools.partial(_front_end_kernel, dims, sample),
        grid=(n // tm,),
        in_specs=in_specs,
        out_specs=[_row_spec(tm, wd) for wd, _ in outs],
        out_shape=[jax.ShapeDtypeStruct((n, wd), dt) for wd, dt in outs],
        compiler_params=_params("parallel"),
        name="front_end_sample" if sample else "front_end_prompt",
    )(*ins)


CONV_HALO = 32
CONV_CHUNK = 32


def _conv_prompt_kernel(conv_w, cur_ref, prev_ref, w_ref, b_ref, lng_ref, lnb_ref, wout_ref, a_ref, ext_ref, act_ref):
    tm = cur_ref.shape[0]
    prev = prev_ref[...]
    ext_ref[0:CONV_HALO, :] = jnp.where(pl.program_id(0) == 0, jnp.zeros_like(prev), prev)
    ext_ref[CONV_HALO:, :] = cur_ref[...]
    first = CONV_HALO - (conv_w - 1)
    for r0 in range(0, tm, CONV_CHUNK):
        acc = jnp.broadcast_to(b_ref[...], (CONV_CHUNK, cur_ref.shape[1]))
        for k in range(conv_w):
            acc = acc + w_ref[k:k + 1, :] * ext_ref[r0 + first + k:r0 + first + k + CONV_CHUNK, :]
        mu = jnp.mean(acc, axis=-1, keepdims=True)
        cen = acc - mu
        var = jnp.mean(cen * cen, axis=-1, keepdims=True)
        y = cen * lax.rsqrt(var + EPS) * lng_ref[...] + lnb_ref[...]
        act_ref[r0:r0 + CONV_CHUNK, :] = jax.nn.silu(y).astype(BF16)
    a_ref[...] = jnp.dot(act_ref[...], wout_ref[...], preferred_element_type=F32)


def _conv_prompt(u, w, tm):
    n, c = u.shape
    conv_w = w["conv_w"].shape[0]
    ratio = tm // CONV_HALO
    return pl.pallas_call(
        functools.partial(_conv_prompt_kernel, conv_w),
        grid=(n // tm,),
        in_specs=[_row_spec(tm, c),
                  pl.BlockSpec((CONV_HALO, c), lambda i: (jnp.maximum(i * ratio - 1, 0), 0)),
                  _const_spec(w["conv_w"].shape), _const_spec((1, c)), _const_spec((1, c)), _const_spec((1, c)),
                  _const_spec(w["w_conv_out"].shape)],
        out_specs=_row_spec(tm, c),
        out_shape=jax.ShapeDtypeStruct((n, c), F32),
        scratch_shapes=[pltpu.VMEM((tm + CONV_HALO, c), F32), pltpu.VMEM((tm, c), BF16)],
        compiler_params=_params("parallel"),
        name="conv_prompt",
    )(u, u, w["conv_w"], w["conv_b"], w["conv_ln_g"], w["conv_ln_b"], w["w_conv_out"])


def _conv_sample_kernel(conv_w, n_new, ext_ref, w_ref, b_ref, lng_ref, lnb_ref, wout_ref, a_ref):
    bb, c = ext_ref.shape[1], ext_ref.shape[2]
    for s in range(n_new):
        acc = jnp.broadcast_to(b_ref[...], (bb, c))
        for k in range(conv_w):
            acc = acc + w_ref[k:k + 1, :] * ext_ref[s + k]
        mu = jnp.mean(acc, axis=-1, keepdims=True)
        cen = acc - mu
        var = jnp.mean(cen * cen, axis=-1, keepdims=True)
        y = cen * lax.rsqrt(var + EPS) * lng_ref[...] + lnb_ref[...]
        a_ref[s] = jnp.dot(jax.nn.silu(y).astype(BF16), wout_ref[...], preferred_element_type=F32)


def _conv_sample(ext_t, w, n_new, bb):
    t, b, c = ext_t.shape
    conv_w = w["conv_w"].shape[0]
    return pl.pallas_call(
        functools.partial(_conv_sample_kernel, conv_w, n_new),
        grid=(b // bb,),
        in_specs=[pl.BlockSpec((t, bb, c), lambda i: (0, i, 0)),
                  _const_spec(w["conv_w"].shape), _const_spec((1, c)), _const_spec((1, c)), _const_spec((1, c)),
                  _const_spec(w["w_conv_out"].shape)],
        out_specs=pl.BlockSpec((n_new, bb, c), lambda i: (0, i, 0)),
        out_shape=jax.ShapeDtypeStruct((n_new, b, c), F32),
        compiler_params=_params("parallel"),
        name="conv_sample",
    )(ext_t, w["conv_w"], w["conv_b"], w["conv_ln_g"], w["conv_ln_b"], w["w_conv_out"])


def _mla_prompt_kernel(q_ref, k_ref, v_ref, o_ref):
    tq = q_ref.shape[0]
    i = pl.program_id(1)
    q = q_ref[...]

    def update(j, carry, masked):
        m, l, acc = carry
        start = pl.multiple_of(j * tq, tq)
        s = _bdot_nt(q, k_ref[pl.ds(start, tq), :])
        if masked:
            row = lax.broadcasted_iota(jnp.int32, (tq, tq), 0)
            col = lax.broadcasted_iota(jnp.int32, (tq, tq), 1)
            s = jnp.where(col <= row, s, NEG_INF)
        m_new = jnp.maximum(m, jnp.max(s, axis=-1, keepdims=True))
        p = jnp.exp(s - m_new)
        alpha = jnp.exp(m - m_new)
        l = alpha * l + jnp.sum(p, axis=-1, keepdims=True)
        acc = alpha * acc + jnp.dot(p.astype(BF16), v_ref[pl.ds(start, tq), :], preferred_element_type=F32)
        return m_new, l, acc

    init = (jnp.full((tq, 1), NEG_INF, F32), jnp.zeros((tq, 1), F32), jnp.zeros((tq, v_ref.shape[1]), F32))
    carry = lax.fori_loop(0, i, lambda j, c: update(j, c, False), init)
    _, l, acc = update(i, carry, True)
    o_ref[...] = (acc / l).astype(BF16)


def _mla_prompt(q, k, v, n_heads, tq):
    n = q.shape[0]
    hp = HEAD_PAD
    return pl.pallas_call(
        _mla_prompt_kernel,
        grid=(n_heads, n // tq),
        in_specs=[pl.BlockSpec((tq, hp), lambda h, i: (i, h)),
                  pl.BlockSpec((n, hp), lambda h, i: (0, h)),
                  pl.BlockSpec((n, hp), lambda h, i: (0, h))],
        out_specs=pl.BlockSpec((tq, hp), lambda h, i: (i, h)),
        out_shape=jax.ShapeDtypeStruct((n, n_heads * hp), BF16),
        compiler_params=_params("parallel", "arbitrary"),
        name="mla_prompt",
    )(q, k, v)


def _mla_sample_kernel(n_new, n_heads, v_head, pt_ref, ql_ref, qp_ref, cn_ref, kn_ref, snt_ref, wuv_ref,
                       ckv_hbm, kpe_hbm, ksc_hbm, o_ref, cbuf, pbuf, sbuf, sems, m_ref, l_ref, acc_ref):
    n_seq, n_pages = pt_ref.shape
    pages = cbuf.shape[1]
    page_size = cbuf.shape[2]
    chunks_per_seq = n_pages // pages
    n_chunks = n_seq * chunks_per_seq
    rows = ql_ref.shape[1]
    tok = pages * page_size
    b = pl.program_id(0)
    c = pl.program_id(1)
    t = b * chunks_per_seq + c
    slot = t % 2

    def copies(t, slot):
        b = t // chunks_per_seq
        c = t % chunks_per_seq
        out = []
        for r in range(pages):
            page = pt_ref[b, c * pages + r]
            out.append(pltpu.make_async_copy(ckv_hbm.at[page], cbuf.at[slot, r], sems.at[0, slot]))
            out.append(pltpu.make_async_copy(kpe_hbm.at[page], pbuf.at[slot, r], sems.at[1, slot]))
            out.append(pltpu.make_async_copy(ksc_hbm.at[page], sbuf.at[slot, r], sems.at[2, slot]))
        return out

    def online_update(s, v_bf):
        m = m_ref[...]
        m_new = jnp.maximum(m, jnp.max(s, axis=-1, keepdims=True))
        p = jnp.exp(s - m_new)
        alpha = jnp.exp(m - m_new)
        l_ref[...] = alpha * l_ref[...] + jnp.sum(p, axis=-1, keepdims=True)
        acc_ref[...] = alpha * acc_ref[...] + jnp.dot(p.astype(BF16), v_bf, preferred_element_type=F32)
        m_ref[...] = m_new

    @pl.when(t == 0)
    def _():
        for cp in copies(0, 0):
            cp.start()

    @pl.when(t + 1 < n_chunks)
    def _():
        for cp in copies(t + 1, 1 - slot):
            cp.start()

    for cp in copies(t, slot):
        cp.wait()

    @pl.when(c == 0)
    def _():
        m_ref[...] = jnp.full(m_ref.shape, NEG_INF, F32)
        l_ref[...] = jnp.zeros(l_ref.shape, F32)
        acc_ref[...] = jnp.zeros(acc_ref.shape, F32)

    eye = (lax.broadcasted_iota(jnp.int32, (n_heads, n_heads), 0)
           == lax.broadcasted_iota(jnp.int32, (n_heads, n_heads), 1)).astype(BF16)
    ql = ql_ref[0]
    qp = qp_ref[0]
    c_bf = cbuf[slot].reshape(tok, cbuf.shape[3]).astype(BF16)
    kp = pbuf[slot].reshape(tok, pbuf.shape[3])
    ks = sbuf[slot].reshape(tok, sbuf.shape[3])
    s = _bdot_nt(ql, c_bf) + _bdot_nt(qp, kp)
    ks_hi = ks.astype(BF16)
    ks_lo = (ks - ks_hi.astype(F32)).astype(BF16)
    ks_t = _bdot_nt(eye, ks_hi) + _bdot_nt(eye, ks_lo)
    s = s * jnp.concatenate([ks_t] * n_new, axis=0)
    online_update(s, c_bf)

    @pl.when(c == chunks_per_seq - 1)
    def _():
        cn = cn_ref[0].astype(BF16).astype(F32)
        kn = kn_ref[0].astype(BF16).astype(F32)
        snt = snt_ref[0]
        qlf = ql.astype(F32)
        qpf = qp.astype(F32)
        q_tok = lax.broadcasted_iota(jnp.int32, (rows, 1), 0) // n_heads
        m = m_ref[...]
        l = l_ref[...]
        acc = acc_ref[...]
        for tk in range(n_new):
            st = (jnp.sum(qlf * cn[tk:tk + 1, :], axis=-1, keepdims=True)
                  + jnp.sum(qpf * kn[tk:tk + 1, :], axis=-1, keepdims=True)) * snt[:, tk:tk + 1]
            vis = q_tok >= tk
            m_new = jnp.where(vis, jnp.maximum(m, st), m)
            p = jnp.where(vis, jnp.exp(st - m_new), 0.0)
            alpha = jnp.exp(m - m_new)
            l = alpha * l + p
            acc = alpha * acc + p.astype(BF16).astype(F32) * cn[tk:tk + 1, :]
            m = m_new
        o_lat = (acc / l).astype(BF16)
        full = jnp.dot(o_lat, wuv_ref[...], preferred_element_type=F32)
        r_head = lax.broadcasted_iota(jnp.int32, full.shape, 0) % n_heads
        c_head = lax.broadcasted_iota(jnp.int32, full.shape, 1) // v_head
        full = jnp.where(r_head == c_head, full, 0.0)
        o_ref[0] = jnp.sum(full.reshape(n_new, n_heads, full.shape[1]), axis=1).astype(BF16)


def _mla_sample(page_table, ql, qp, cn, kn, snt, w_uv_flat, ckv, kpe, ksc, n_new, n_heads, v_head):
    n_seq, rows = ql.shape[:2]
    page_size = ckv.shape[1]
    pages = min(PAGES_PER_CHUNK, page_table.shape[1])
    per_seq = lambda a: pl.BlockSpec((1,) + a.shape[1:], lambda b, c, pt: (b, 0, 0))
    grid_spec = pltpu.PrefetchScalarGridSpec(
        num_scalar_prefetch=1,
        grid=(n_seq, page_table.shape[1] // pages),
        in_specs=[per_seq(ql), per_seq(qp), per_seq(cn), per_seq(kn), per_seq(snt),
                  pl.BlockSpec(w_uv_flat.shape, lambda b, c, pt: (0, 0)),
                  pl.BlockSpec(memory_space=pl.ANY), pl.BlockSpec(memory_space=pl.ANY),
                  pl.BlockSpec(memory_space=pl.ANY)],
        out_specs=pl.BlockSpec((1, n_new, n_heads * v_head), lambda b, c, pt: (b, 0, 0)),
        scratch_shapes=[pltpu.VMEM((2, pages, page_size, ckv.shape[2]), F32),
                        pltpu.VMEM((2, pages, page_size, kpe.shape[2]), F32),
                        pltpu.VMEM((2, pages, page_size, ksc.shape[2]), F32),
                        pltpu.SemaphoreType.DMA((3, 2)),
                        pltpu.VMEM((rows, 1), F32), pltpu.VMEM((rows, 1), F32),
                        pltpu.VMEM((rows, ckv.shape[2]), F32)])
    return pl.pallas_call(
        functools.partial(_mla_sample_kernel, n_new, n_heads, v_head),
        grid_spec=grid_spec,
        out_shape=jax.ShapeDtypeStruct((n_seq, n_new, n_heads * v_head), BF16),
        compiler_params=_params("arbitrary", "arbitrary"),
        name="mla_sample",
    )(page_table, ql, qp, cn, kn, snt, w_uv_flat, ckv, kpe, ksc)


def _mem_kv_kernel(mem_heads, mem_ref, g_ref, wk_ref, wv_ref, gk_ref, k_ref, v_ref):
    m = _rms(mem_ref[...], g_ref[...]).astype(BF16)
    k = jnp.dot(m, wk_ref[...], preferred_element_type=F32)
    d = k.shape[1] // mem_heads
    for hd in range(mem_heads):
        k_ref[:, hd * d:(hd + 1) * d] = _rms(k[:, hd * d:(hd + 1) * d], gk_ref[...])
    v_ref[...] = jnp.dot(m, wv_ref[...], preferred_element_type=F32)


def _mem_kv(mem, w, mem_heads):
    n, d = mem.shape
    ins = [mem, w["gmem"], w["w_mk"], w["w_mv"], w["gmk"]]
    return pl.pallas_call(
        functools.partial(_mem_kv_kernel, mem_heads),
        grid=(1,),
        in_specs=[_const_spec(a.shape) for a in ins],
        out_specs=[_const_spec((n, d))] * 2,
        out_shape=[jax.ShapeDtypeStruct((n, d), F32)] * 2,
        compiler_params=_params("arbitrary"),
        name="mem_kv",
    )(*ins)


def _mem_attend_kernel(mem_heads, q_ref, k_ref, v_ref, o_ref):
    q = q_ref[0]
    k = k_ref[0].astype(BF16)
    v = v_ref[0].astype(BF16)
    d = q.shape[1] // mem_heads
    for hd in range(mem_heads):
        sl = slice(hd * d, (hd + 1) * d)
        s = _bdot_nt(q[:, sl], k[:, sl])
        p = jnp.exp(s - jnp.max(s, axis=-1, keepdims=True))
        p = p / jnp.sum(p, axis=-1, keepdims=True)
        o_ref[0, :, sl] = jnp.dot(p.astype(BF16), v[:, sl], preferred_element_type=F32).astype(BF16)


def _mem_attend(q, k, v, mem_heads, shared_kv):
    g, tq, d = q.shape
    n_mem = k.shape[1]
    kv_map = (lambda i: (0, 0, 0)) if shared_kv else (lambda i: (i, 0, 0))
    return pl.pallas_call(
        functools.partial(_mem_attend_kernel, mem_heads),
        grid=(g,),
        in_specs=[pl.BlockSpec((1, tq, d), lambda i: (i, 0, 0)),
                  pl.BlockSpec((1, n_mem, d), kv_map), pl.BlockSpec((1, n_mem, d), kv_map)],
        out_specs=pl.BlockSpec((1, tq, d), lambda i: (i, 0, 0)),
        out_shape=jax.ShapeDtypeStruct((g, tq, d), BF16),
        compiler_params=_params("parallel"),
        name="mem_attend_prompt" if shared_kv else "mem_attend_sample",
    )(q, k, v)


def _merge_kernel(x_ref, a_ref, mla_ref, mem_ref, g1_ref, wg_ref, wo_ref, wmo_ref, wout_ref, h_ref):
    x = x_ref[...]
    d = x.shape[1]
    gates = jax.nn.sigmoid(jnp.dot(_rms(x, g1_ref[...]).astype(BF16), wg_ref[...], preferred_element_type=F32))
    br_b = jnp.dot(mla_ref[...], wo_ref[...], preferred_element_type=F32)
    br_c = jnp.dot(mem_ref[...], wmo_ref[...], preferred_element_type=F32)
    mix = gates[:, :d] * a_ref[...] + gates[:, d:2 * d] * br_b + gates[:, 2 * d:] * br_c
    h_ref[...] = x + jnp.dot(mix.astype(BF16), wout_ref[...], preferred_element_type=F32)


def _merge(x, a, mla_o, mem_o, w, w_o, tm):
    n, d = x.shape
    ins = [x, a, mla_o, mem_o, w["g1"], w["w_gate"], w_o, w["w_mo"], w["w_out"]]
    return pl.pallas_call(
        _merge_kernel,
        grid=(n // tm,),
        in_specs=[_row_spec(tm, d), _row_spec(tm, d), _row_spec(tm, mla_o.shape[1]), _row_spec(tm, d)]
        + [_const_spec(a_.shape) for a_ in ins[4:]],
        out_specs=_row_spec(tm, d),
        out_shape=jax.ShapeDtypeStruct((n, d), F32),
        compiler_params=_params("parallel"),
        name="merge",
    )(*ins)


FFN_HALO = 8
FFN_CHUNK = 64


def _ffn_prompt_kernel(conv_w, x_ref, g2_ref, wup_ref, cw_ref, cb_ref, wdown_ref, y_ref, tail_ref, ext_ref, act_ref):
    tm = x_ref.shape[0]
    d_ff = wdown_ref.shape[0]

    @pl.when(pl.program_id(0) == 0)
    def _():
        ext_ref[0:FFN_HALO, :] = jnp.zeros((FFN_HALO, ext_ref.shape[1]), F32)

    x = x_ref[...]
    ext_ref[FFN_HALO:, :] = jnp.dot(_rms(x, g2_ref[...]).astype(BF16), wup_ref[...], preferred_element_type=F32)
    first = FFN_HALO - (conv_w - 1)
    for r0 in range(0, tm, FFN_CHUNK):
        hc = jnp.broadcast_to(cb_ref[...], (FFN_CHUNK, ext_ref.shape[1]))
        for k in range(conv_w):
            hc = hc + cw_ref[k:k + 1, :] * ext_ref[r0 + first + k:r0 + first + k + FFN_CHUNK, :]
        act_ref[r0:r0 + FFN_CHUNK, :] = (jax.nn.silu(hc[:, :d_ff]) * hc[:, d_ff:]).astype(BF16)
    y_ref[...] = x + jnp.dot(act_ref[...], wdown_ref[...], preferred_element_type=F32)
    tail = ext_ref[tm:tm + FFN_HALO, :]
    tail_ref[...] = tail
    ext_ref[0:FFN_HALO, :] = tail


def _ffn_prompt(x, w, tm):
    n, d = x.shape
    d_up = w["w_up"].shape[1]
    conv_w = w["ffn_conv_w"].shape[0]
    ins = [x, w["g2"], w["w_up"], w["ffn_conv_w"], w["ffn_conv_b"], w["w_down"]]
    return pl.pallas_call(
        functools.partial(_ffn_prompt_kernel, conv_w),
        grid=(n // tm,),
        in_specs=[_row_spec(tm, d)] + [_const_spec(a.shape) for a in ins[1:]],
        out_specs=[_row_spec(tm, d), _const_spec((FFN_HALO, d_up))],
        out_shape=[jax.ShapeDtypeStruct((n, d), F32), jax.ShapeDtypeStruct((FFN_HALO, d_up), F32)],
        scratch_shapes=[pltpu.VMEM((tm + FFN_HALO, d_up), F32), pltpu.VMEM((tm, d_up // 2), BF16)],
        compiler_params=_params("arbitrary"),
        name="ffn_prompt",
    )(*ins)


def _ffn_sample_kernel(conv_w, n_new, x_ref, hist_ref, g2_ref, wup_ref, cw_ref, cb_ref, wdown_ref, y_ref, up_ref):
    n_seq = hist_ref.shape[1]
    d_ff = wdown_ref.shape[0]
    x = x_ref[...]
    up_ref[...] = jnp.dot(_rms(x, g2_ref[...]).astype(BF16), wup_ref[...], preferred_element_type=F32)

    def ext(j):
        if j < conv_w - 1:
            return hist_ref[j]
        j -= conv_w - 1
        return up_ref[j * n_seq:(j + 1) * n_seq, :]

    for s in range(n_new):
        hc = jnp.broadcast_to(cb_ref[...], (n_seq, up_ref.shape[1]))
        for k in range(conv_w):
            hc = hc + cw_ref[k:k + 1, :] * ext(s + k)
        act = (jax.nn.silu(hc[:, :d_ff]) * hc[:, d_ff:]).astype(BF16)
        rows = slice(s * n_seq, (s + 1) * n_seq)
        y_ref[rows, :] = x[rows, :] + jnp.dot(act, wdown_ref[...], preferred_element_type=F32)


def _ffn_sample(x, hist_t, w, n_new):
    n, d = x.shape
    d_up = w["w_up"].shape[1]
    conv_w = w["ffn_conv_w"].shape[0]
    ins = [x, hist_t, w["g2"], w["w_up"], w["ffn_conv_w"], w["ffn_conv_b"], w["w_down"]]
    return pl.pallas_call(
        functools.partial(_ffn_sample_kernel, conv_w, n_new),
        grid=(1,),
        in_specs=[_const_spec(a.shape) for a in ins],
        out_specs=[_const_spec((n, d)), _const_spec((n, d_up))],
        out_shape=[jax.ShapeDtypeStruct((n, d), F32), jax.ShapeDtypeStruct((n, d_up), F32)],
        compiler_params=_params("arbitrary"),
        name="ffn_sample",
    )(*ins)


def _pad_heads(w3, width):
    k, h, d = w3.shape
    return jnp.concatenate([w3, jnp.zeros((k, h, width - d), w3.dtype)], axis=-1).reshape(k, h * width)


def _rot_half(w, half):
    return jnp.concatenate([-w[..., half:], w[..., :half]], axis=-1)


def _swap_half(g, half):
    return jnp.concatenate([g[..., half:], g[..., :half]], axis=-1)


def _lane_place(v, offset):
    k, d = v.shape
    return jnp.concatenate([jnp.zeros((k, offset), v.dtype), v, jnp.zeros((k, HEAD_PAD - offset - d), v.dtype)], axis=-1)


def _prep_weights(norm1_g, w_in, q_a_norm_g, w_uq, kv_a_norm_g, w_uk, w_uv, q_norm_g, k_norm_g, w_o_mla, conv_w, conv_b,
                  conv_ln_g, conv_ln_b, w_conv_out, mem_norm_g, w_mk, w_mv, mq_norm_g, mk_norm_g, w_mo, w_out, norm2_g,
                  w_up, ffn_conv_w, ffn_conv_b, w_down, rope, mem_heads):
    hp = HEAD_PAD
    d = w_in.shape[0]
    c_conv = conv_w.shape[1]
    q_lora = w_uq.shape[0]
    kv_lora, n_heads, nope = w_uk.shape
    v_head = w_uv.shape[2]
    half = rope // 2
    o_qa = 2 * c_conv
    o_kva = o_qa + q_lora
    o_mq = o_kva + kv_lora + rope
    o_gate = o_mq + w_mk.shape[1]
    w_pe = w_in[:, o_kva + kv_lora:o_mq]
    w_main = jnp.concatenate([w_in[:, :o_kva + kv_lora], _lane_place(w_pe, nope), _lane_place(_rot_half(w_pe, half), nope),
                              w_in[:, o_mq:o_gate]], axis=1).astype(BF16)
    uq3 = w_uq.reshape(q_lora, n_heads, nope + rope)
    uq_rot3 = jnp.concatenate([jnp.zeros((q_lora, n_heads, nope), F32), _rot_half(uq3[..., nope:], half)], axis=-1)
    row = lambda g: g.reshape(1, -1)
    sel = (jnp.arange(n_heads * hp)[:, None]
           == ((jnp.arange(n_heads * rope) // rope) * hp + nope + jnp.arange(n_heads * rope) % rope)[None, :])
    w_abs = jnp.transpose(w_uk, (1, 2, 0))
    w_abs = jnp.concatenate([w_abs, jnp.zeros((n_heads, hp - nope, kv_lora), F32)], axis=1)
    w_o3 = w_o_mla.reshape(n_heads, v_head, d)
    w_o_pad = jnp.concatenate([w_o3, jnp.zeros((n_heads, hp - v_head, d), F32)], axis=1).reshape(n_heads * hp, d)
    return {
        "g1": row(norm1_g), "w_main": w_main, "w_gate": w_in[:, o_gate:].astype(BF16),
        "gqa": row(q_a_norm_g), "w_uq_pad": _pad_heads(uq3, hp).astype(BF16), "w_uq_rot": _pad_heads(uq_rot3, hp).astype(BF16),
        "gq_pad": _lane_place(row(q_norm_g), 0),
        "gq_rot": _lane_place(_swap_half(row(q_norm_g)[:, nope:], half), nope),
        "gkv": row(kv_a_norm_g), "w_uk_pad": _pad_heads(w_uk, hp).astype(BF16), "w_uv_pad": _pad_heads(w_uv, hp).astype(BF16),
        "gkn_pad": _lane_place(row(k_norm_g)[:, :nope], 0),
        "gkp_pad": _lane_place(row(k_norm_g)[:, nope:], nope),
        "gkp_rot": _lane_place(_swap_half(row(k_norm_g)[:, nope:], half), nope),
        "gmq": row(mq_norm_g), "w_abs": w_abs.astype(BF16), "sel_pe": sel.astype(BF16),
        "w_uv_flat": w_uv.reshape(kv_lora, n_heads * v_head).astype(BF16),
        "w_o_pad": w_o_pad.astype(BF16), "w_o": w_o_mla.astype(BF16),
        "conv_w": conv_w, "conv_b": row(conv_b), "conv_ln_g": row(conv_ln_g), "conv_ln_b": row(conv_ln_b),
        "w_conv_out": w_conv_out.astype(BF16),
        "gmem": row(mem_norm_g), "w_mk": w_mk.astype(BF16), "w_mv": w_mv.astype(BF16), "gmk": row(mk_norm_g),
        "w_mo": w_mo.astype(BF16), "w_out": w_out.astype(BF16),
        "g2": row(norm2_g), "w_up": w_up.astype(BF16), "ffn_conv_w": ffn_conv_w, "ffn_conv_b": row(ffn_conv_b),
        "w_down": w_down.astype(BF16),
    }


def _rope_tables(pos, rope, nope):
    half = rope // 2
    inv_freq = ROPE_THETA ** (-jnp.arange(half, dtype=F32) / half)
    ang = pos.astype(F32)[:, None] * inv_freq[None, :]
    cos, sin = jnp.cos(ang), jnp.sin(ang)
    n = pos.shape[0]
    pad = jnp.zeros((n, HEAD_PAD - nope - rope), F32)
    cos_t = jnp.concatenate([jnp.ones((n, nope), F32), cos, cos, pad], axis=-1)
    sin_t = jnp.concatenate([jnp.zeros((n, nope), F32), sin, sin, pad], axis=-1)
    return cos_t, sin_t


def kernel(x_prompt, x_sample, mem_prompt, cache_ckv, cache_kpe, cache_kscale, page_table, state_conv, state_ffn, cache_mem_k, cache_mem_v, norm1_g, w_in, q_a_norm_g, w_uq, kv_a_norm_g, w_uk, w_uv, q_norm_g, k_norm_g, w_o_mla, conv_w, conv_b, conv_ln_g, conv_ln_b, w_conv_out, mem_norm_g, w_mk, w_mv, mq_norm_g, mk_norm_g, w_mo, w_out, norm2_g, w_up, ffn_conv_w, ffn_conv_b, w_down):
    depth = w_in.shape[0]
    assert depth == 1, "single trunk layer"
    b_p, s_p, d = x_prompt.shape
    assert b_p == 1, "one prompt sequence"
    n_seq, n_new, _ = x_sample.shape
    page_size = cache_ckv.shape[2]
    n_past = page_table.shape[1] * page_size
    kv_lora, n_heads, nope = w_uk.shape[1:]
    rope = cache_kpe.shape[-1]
    v_head = w_uv.shape[-1]
    qk_dim = nope + rope
    mem_heads, mem_dim = cache_mem_k.shape[-2:]
    n_mem = mem_prompt.shape[1]
    c_conv = conv_w.shape[-1]
    conv_width = conv_w.shape[1]
    ffn_width = ffn_conv_w.shape[1]
    d_up = w_up.shape[-1]
    hp = HEAD_PAD
    dims = (c_conv, w_uq.shape[1], kv_lora, n_heads, nope, rope, mem_heads, mem_dim)

    w = _prep_weights(norm1_g[0], w_in[0], q_a_norm_g[0], w_uq[0], kv_a_norm_g[0], w_uk[0], w_uv[0], q_norm_g[0],
                      k_norm_g[0], w_o_mla[0], conv_w[0], conv_b[0], conv_ln_g[0], conv_ln_b[0], w_conv_out[0],
                      mem_norm_g[0], w_mk[0], w_mv[0], mq_norm_g[0], mk_norm_g[0], w_mo[0], w_out[0], norm2_g[0],
                      w_up[0], ffn_conv_w[0], ffn_conv_b[0], w_down[0], rope, mem_heads)

    tm = min(ROW_TILE, s_p)
    xp = x_prompt.reshape(s_p, d)
    cos_p, sin_p = _rope_tables(jnp.arange(s_p), rope, nope)
    u_p, ckv_p, kpe_p, ksc_p, mq_p, q_p, k_p, v_p = _front_end(xp, cos_p, sin_p, w, dims, False, tm)
    a_p = _conv_prompt(u_p, w, tm)
    mla_p = _mla_prompt(q_p, k_p, v_p, n_heads, min(ATTN_TILE, s_p))
    mk_p, mv_p = _mem_kv(mem_prompt.reshape(n_mem, d), w, mem_heads)
    mem_o_p = _mem_attend(mq_p.reshape(s_p // tm, tm, d), mk_p[None], mv_p[None], mem_heads, True).reshape(s_p, d)
    h_p = _merge(xp, a_p, mla_p, mem_o_p, w, w["w_o_pad"], tm)
    y_p, ffn_tail = _ffn_prompt(h_p, w, tm)

    n_s = n_seq * n_new
    xs = jnp.transpose(x_sample, (1, 0, 2)).reshape(n_s, d)
    pos_s = jnp.repeat(n_past + jnp.arange(n_new), n_seq)
    cos_s, sin_s = _rope_tables(pos_s, rope, nope)
    tms = min(ROW_TILE, n_s)
    u_s, ckv_s, kpe_s, ksc_s, mq_s, qlat_s, qpe_s, _ = _front_end(xs, cos_s, sin_s, w, dims, True, tms)

    def seq_major(a):
        return jnp.transpose(a.reshape(n_new, n_seq, -1), (1, 0, 2))

    ext_t = jnp.concatenate([jnp.transpose(state_conv[0], (1, 0, 2)), u_s.reshape(n_new, n_seq, c_conv)], axis=0)
    a_s = _conv_sample(ext_t, w, n_new, min(32, n_seq)).reshape(n_s, d)

    ckv_s_b = seq_major(ckv_s)
    kpe_s_b = seq_major(kpe_s)[..., nope:nope + rope]
    ksc_s_b = seq_major(ksc_s)[..., :n_heads]
    pad_rows = (-n_new) % 8
    pad_new = lambda a: jnp.pad(a, ((0, 0), (0, pad_rows), (0, 0)))
    snt = jnp.tile(jnp.transpose(ksc_s_b, (0, 2, 1)), (1, n_new, 1))
    snt = jnp.pad(snt, ((0, 0), (0, 0), (0, pad_rows)))
    ql = seq_major(qlat_s).reshape(n_seq, n_new * n_heads, kv_lora)
    qp = seq_major(qpe_s).reshape(n_seq, n_new * n_heads, rope)
    mla_s = _mla_sample(page_table, ql, qp, pad_new(ckv_s_b), pad_new(kpe_s_b), snt, w["w_uv_flat"],
                        cache_ckv.reshape(cache_ckv.shape[1:]), cache_kpe.reshape(cache_kpe.shape[1:]),
                        cache_kscale.reshape(cache_kscale.shape[1:]), n_new, n_heads, v_head)
    mla_s_t = jnp.transpose(mla_s, (1, 0, 2)).reshape(n_s, n_heads * v_head)

    mem_o_s = _mem_attend(seq_major(mq_s), cache_mem_k[0].reshape(n_seq, n_mem, d), cache_mem_v[0].reshape(n_seq, n_mem, d),
                          mem_heads, False)
    mem_o_s_t = jnp.transpose(mem_o_s, (1, 0, 2)).reshape(n_s, d)
    h_s = _merge(xs, a_s, mla_s_t, mem_o_s_t, w, w["w_o"], tms)
    y_s, up_s = _ffn_sample(h_s, jnp.transpose(state_ffn[0], (1, 0, 2)), w, n_new)

    p_ckv = ckv_p.reshape(1, 1, s_p, kv_lora)
    p_kpe = kpe_p[:, nope:nope + rope].reshape(1, 1, s_p, rope)
    p_ksc = ksc_p[:, :n_heads].reshape(1, 1, s_p, n_heads)
    p_conv = u_p[s_p - (conv_width - 1):].reshape(1, 1, conv_width - 1, c_conv)
    p_ffn = ffn_tail[FFN_HALO - (ffn_width - 1):].reshape(1, 1, ffn_width - 1, d_up)
    p_mem_k = mk_p.reshape(1, 1, n_mem, mem_heads, mem_dim)
    p_mem_v = mv_p.reshape(1, 1, n_mem, mem_heads, mem_dim)
    s_conv = jnp.transpose(ext_t[n_new:], (1, 0, 2))[None]
    ffn_ext = jnp.concatenate([state_ffn[0], seq_major(up_s)], axis=1)
    s_ffn = ffn_ext[:, n_new:][None]
    return (y_p.reshape(1, s_p, d), seq_major(y_s), p_ckv, p_kpe, p_ksc, p_conv, p_ffn, p_mem_k, p_mem_v,
            ckv_s_b[None], kpe_s_b[None], ksc_s_b[None], s_conv, s_ffn)
```

```python
import functools

import jax
import jax.numpy as jnp
from jax import lax
from jax.experimental import pallas as pl
from jax.experimental.pallas import tpu as pltpu

F32 = jnp.float32
BF16 = jnp.bfloat16
EPS = 1e-6
ROPE_THETA = 10000.0
HEAD_PAD = 128
VMEM_LIMIT = 56 * 1024 * 1024
ROW_TILE = 512
ATTN_Q_TILE = 1024
ATTN_KV_TILE = 1024
PAGES_PER_CHUNK = 32
NEG_INF = float("-inf")
LOG2E = 1.4426950408889634


def _const_spec(shape):
    nd = len(shape)
    return pl.BlockSpec(shape, lambda *_: (0,) * nd, pipeline_mode=pl.Buffered(1))


def _whole_out_spec(shape):
    nd = len(shape)
    return pl.BlockSpec(shape, lambda *_: (0,) * nd)


def _row_spec(tm, width):
    return pl.BlockSpec((tm, width), lambda i: (i, 0))


def _params(*sem):
    return pltpu.CompilerParams(dimension_semantics=sem, vmem_limit_bytes=VMEM_LIMIT)


def _bdot(a, b):
    return jnp.dot(a.astype(BF16), b.astype(BF16), preferred_element_type=F32)


def _bdot_nt(a, b):
    return lax.dot_general(a.astype(BF16), b.astype(BF16), (((1,), (1,)), ((), ())), preferred_element_type=F32)


def _rms(x, g):
    return x * lax.rsqrt(jnp.mean(x * x, axis=-1, keepdims=True) + EPS) * g


def _front_end_kernel(dims, sample, x_ref, cos_ref, sin_ref, g1_ref, wmain_ref, gqa_ref, wuq_ref, wuqr_ref,
                      gq_ref, gqr_ref, gkv_ref, wuk_ref, gkn_ref, gkp_ref, gkpr_ref, gmq_ref, *rest):
    c_conv, q_lora, kv_lora, n_heads, nope, rope, v_head, mem_heads, mem_dim = dims
    qk_dim = nope + rope
    sm_scale = qk_dim ** -0.5 * LOG2E
    if sample:
        wabs_ref, sel_ref, u_ref, ckv_ref, kpe_ref, ksc_ref, mq_ref, qlat_ref, qpe_ref, qs_ref = rest
    else:
        wuv_ref, u_ref, ckv_ref, kpe_ref, ksc_ref, mq_ref, q_ref, k_ref, v_ref = rest
    hp = HEAD_PAD
    h = _rms(x_ref[...], g1_ref[...]).astype(BF16)
    z = jnp.dot(h, wmain_ref[...], preferred_element_type=F32)
    o_qa = 2 * c_conv
    o_c = o_qa + q_lora
    o_pe = o_c + kv_lora
    o_per = o_pe + hp
    o_mq = o_per + hp
    u_ref[...] = z[:, :c_conv] * jax.nn.sigmoid(z[:, c_conv:o_qa])

    cos_t = cos_ref[...]
    sin_t = sin_ref[...]
    qa = _rms(z[:, o_qa:o_c], gqa_ref[...]).astype(BF16)
    q = jnp.dot(qa, wuq_ref[...], preferred_element_type=F32)
    qr = jnp.dot(qa, wuqr_ref[...], preferred_element_type=F32)
    g_cos = gq_ref[...] * cos_t
    g_sin = gqr_ref[...] * sin_t
    q_out = qs_ref if sample else q_ref
    for hd in range(n_heads):
        sl = slice(hd * hp, (hd + 1) * hp)
        qh = q[:, sl]
        r = lax.rsqrt(jnp.sum(qh * qh, axis=-1, keepdims=True) / qk_dim + EPS)
        qn = r * (qh * g_cos + qr[:, sl] * g_sin) * sm_scale
        q_out[:, sl] = qn.astype(BF16)
        if sample:
            qlat_ref[:, hd * kv_lora:(hd + 1) * kv_lora] = _bdot(qn * gkn_ref[...], wabs_ref[hd]).astype(BF16)
    if sample:
        qpe_ref[...] = jnp.dot(qs_ref[...], sel_ref[...], preferred_element_type=F32).astype(BF16)

    c_kv = _rms(z[:, o_c:o_pe], gkv_ref[...])
    ckv_ref[...] = c_kv
    pe = z[:, o_pe:o_per]
    k_pe = pe * (gkp_ref[...] * cos_t) + z[:, o_per:o_mq] * (gkpr_ref[...] * sin_t)
    kpe_ref[...] = k_pe
    ss_pe = jnp.sum(pe * pe, axis=-1, keepdims=True)
    c_bf = c_kv.astype(BF16)
    k_nope = jnp.dot(c_bf, wuk_ref[...], preferred_element_type=F32)
    lane = lax.broadcasted_iota(jnp.int32, (x_ref.shape[0], hp), 1)
    ksc_all = jnp.zeros((x_ref.shape[0], hp), F32)
    for hd in range(n_heads):
        sl = slice(hd * hp, (hd + 1) * hp)
        kn = k_nope[:, sl]
        ksc = lax.rsqrt((jnp.sum(kn * kn, axis=-1, keepdims=True) + ss_pe) / qk_dim + EPS)
        ksc_all = jnp.where(lane == hd, ksc, ksc_all)
        if not sample:
            k_ref[:, sl] = ((kn * gkn_ref[...] + k_pe) * ksc).astype(BF16)
    ksc_ref[...] = ksc_all
    if not sample:
        v = jnp.dot(c_bf, wuv_ref[...], preferred_element_type=F32)
        head_lane = lax.broadcasted_iota(jnp.int32, v.shape, 1) % hp
        v_ref[...] = jnp.where(head_lane == v_head, 1.0, v).astype(BF16)

    for hd in range(mem_heads):
        sl = slice(hd * mem_dim, (hd + 1) * mem_dim)
        mq_ref[:, sl] = (_rms(z[:, o_mq + hd * mem_dim:o_mq + (hd + 1) * mem_dim], gmq_ref[...])
                         * (mem_dim ** -0.5)).astype(BF16)


def _front_end(x, cos_t, sin_t, w, dims, sample, tm):
    n, d = x.shape
    c_conv, q_lora, kv_lora, n_heads, nope, rope, v_head, mem_heads, mem_dim = dims
    hp = HEAD_PAD
    ins =[x, cos_t, sin_t, w["g1"], w["w_main"], w["gqa"], w["w_uq_pad"], w["w_uq_rot"], w["gq_pad"], w["gq_rot"],
           w["gkv"], w["w_uk_pad"], w["gkn_pad"], w["gkp_pad"], w["gkp_rot"], w["gmq"]]
    ins += [w["w_abs"], w["sel_pe"]] if sample else [w["w_uv_pad"]]
    in_specs = [_row_spec(tm, d), _row_spec(tm, hp), _row_spec(tm, hp)] + [_const_spec(a.shape) for a in ins[3:]]
    outs = [(c_conv, F32), (kv_lora, F32), (hp, F32), (hp, F32), (mem_heads * mem_dim, BF16)]
    if sample:
        outs += [(n_heads * kv_lora, BF16), (n_heads * rope, BF16), (n_heads * hp, BF16)]
    else:
        outs += [(n_heads * hp, BF16)] * 3
    return pl.pallas_call(
        functools.partial(_front_end_kernel, dims, sample),
        grid=(n // tm,),
        in_specs=in_specs,
        out_specs=[_row_spec(tm, wd) for wd, _ in outs],
        out_shape=[jax.ShapeDtypeStruct((n, wd), dt) for wd, dt in outs],
        compiler_params=_params("parallel"),
        name="front_end_sample" if sample else "front_end_prompt",
    )(*ins)


CONV_HALO = 32
CONV_CHUNK = 16
SUBLANES = 8


def _conv_prompt_kernel(conv_w, cur_ref, prev_ref, w_ref, b_ref, lng_ref, lnb_ref, wout_ref, a_ref, ext_ref, sh_ref,
                        act_ref):
    tm = cur_ref.shape[0]
    prev = prev_ref[...]
    ext_ref[0:CONV_HALO, :] = jnp.where(pl.program_id(0) == 0, jnp.zeros_like(prev), prev)
    ext_ref[CONV_HALO:, :] = cur_ref[...]
    span = sh_ref.shape[1]
    for phase in range(1, SUBLANES):
        sh_ref[phase - 1] = ext_ref[phase:phase + span, :]
    first = CONV_HALO - (conv_w - 1)
    groups = CONV_CHUNK // SUBLANES
    for r0 in range(0, tm, CONV_CHUNK):
        accs = [b_ref[...]] * groups
        for k in range(conv_w):
            phase = (first + k) % SUBLANES
            base = r0 + first + k - phase
            w8 = w_ref[k]
            for g in range(groups):
                lo = base + g * SUBLANES
                tap = ext_ref[lo:lo + SUBLANES, :] if phase == 0 else sh_ref[phase - 1, lo:lo + SUBLANES, :]
                accs[g] = accs[g] + w8 * tap
        acc = jnp.concatenate(accs, axis=0)
        mu = jnp.mean(acc, axis=-1, keepdims=True)
        cen = acc - mu
        var = jnp.mean(cen * cen, axis=-1, keepdims=True)
        y = cen * lax.rsqrt(var + EPS) * lng_ref[...] + lnb_ref[...]
        act_ref[r0:r0 + CONV_CHUNK, :] = jax.nn.silu(y).astype(BF16)
    a_ref[...] = jnp.dot(act_ref[...], wout_ref[...], preferred_element_type=F32)


def _conv_prompt(u, w, tm):
    n, c = u.shape
    conv_w = w["conv_w"].shape[0]
    ratio = tm // CONV_HALO
    return pl.pallas_call(
        functools.partial(_conv_prompt_kernel, conv_w),
        grid=(n // tm,),
        in_specs=[_row_spec(tm, c),
                  pl.BlockSpec((CONV_HALO, c), lambda i: (jnp.maximum(i * ratio - 1, 0), 0)),
                  _const_spec(w["conv_w8"].shape), _const_spec((SUBLANES, c)), _const_spec((1, c)), _const_spec((1, c)),
                  _const_spec(w["w_conv_out"].shape)],
        out_specs=_row_spec(tm, c),
        out_shape=jax.ShapeDtypeStruct((n, c), F32),
        scratch_shapes=[pltpu.VMEM((tm + CONV_HALO, c), F32),
                        pltpu.VMEM((SUBLANES - 1, tm + CONV_HALO - SUBLANES, c), F32),
                        pltpu.VMEM((tm, c), BF16)],
        compiler_params=_params("parallel"),
        name="conv_prompt",
    )(u, u, w["conv_w8"], w["conv_b8"], w["conv_ln_g"], w["conv_ln_b"], w["w_conv_out"])


def _conv_sample_kernel(conv_w, n_new, ext_ref, w_ref, b_ref, lng_ref, lnb_ref, wout_ref, a_ref):
    bb, c = ext_ref.shape[1], ext_ref.shape[2]
    for s in range(n_new):
        acc = jnp.broadcast_to(b_ref[...], (bb, c))
        for k in range(conv_w):
            acc = acc + w_ref[k:k + 1, :] * ext_ref[s + k]
        mu = jnp.mean(acc, axis=-1, keepdims=True)
        cen = acc - mu
        var = jnp.mean(cen * cen, axis=-1, keepdims=True)
        y = cen * lax.rsqrt(var + EPS) * lng_ref[...] + lnb_ref[...]
        a_ref[s] = jnp.dot(jax.nn.silu(y).astype(BF16), wout_ref[...], preferred_element_type=F32)


def _conv_sample(ext_t, w, n_new, bb):
    t, b, c = ext_t.shape
    conv_w = w["conv_w"].shape[0]
    return pl.pallas_call(
        functools.partial(_conv_sample_kernel, conv_w, n_new),
        grid=(b // bb,),
        in_specs=[pl.BlockSpec((t, bb, c), lambda i: (0, i, 0)),
                  _const_spec(w["conv_w"].shape), _const_spec((1, c)), _const_spec((1, c)), _const_spec((1, c)),
                  _const_spec(w["w_conv_out"].shape)],
        out_specs=pl.BlockSpec((n_new, bb, c), lambda i: (0, i, 0)),
        out_shape=jax.ShapeDtypeStruct((n_new, b, c), F32),
        compiler_params=_params("parallel"),
        name="conv_sample",
    )(ext_t, w["conv_w"], w["conv_b"], w["conv_ln_g"], w["conv_ln_b"], w["w_conv_out"])


def _mla_prompt_kernel(v_head, tk, q_ref, k_ref, v_ref, o_ref):
    tq = q_ref.shape[0]
    per_tile = tq // tk
    i = pl.program_id(1)
    q = q_ref[...]

    def update(j, carry, diag):
        m, acc = carry
        start = pl.multiple_of(j * tk, tk)
        s = _bdot_nt(q, k_ref[pl.ds(start, tk), :])
        if diag is not None:
            row = lax.broadcasted_iota(jnp.int32, (tq, tk), 0)
            col = lax.broadcasted_iota(jnp.int32, (tq, tk), 1) + diag * tk
            s = jnp.where(col <= row, s, NEG_INF)
        m_new = jnp.maximum(m, jnp.max(s, axis=-1, keepdims=True))
        p = jnp.exp2(s - m_new).astype(BF16)
        acc = jnp.exp2(m - m_new) * acc + jnp.dot(p, v_ref[pl.ds(start, tk), :], preferred_element_type=F32)
        return m_new, acc

    carry = (jnp.full((tq, 1), NEG_INF, F32), jnp.zeros((tq, v_ref.shape[1]), F32))
    carry = lax.fori_loop(0, i * per_tile, lambda j, c: update(j, c, None), carry)
    for d in range(per_tile):
        carry = update(i * per_tile + d, carry, d)
    acc = carry[1]
    o_ref[...] = (acc / acc[:, v_head:v_head + 1]).astype(BF16)


def _mla_prompt(q, k, v, n_heads, v_head, tq, tk):
    n = q.shape[0]
    hp = HEAD_PAD
    return pl.pallas_call(
        functools.partial(_mla_prompt_kernel, v_head, tk),
        grid=(n_heads, n // tq),
        in_specs=[pl.BlockSpec((tq, hp), lambda h, i: (i, h)),
                  pl.BlockSpec((n, hp), lambda h, i: (0, h)),
                  pl.BlockSpec((n, hp), lambda h, i: (0, h))],
        out_specs=pl.BlockSpec((tq, hp), lambda h, i: (i, h)),
        out_shape=jax.ShapeDtypeStruct((n, n_heads * hp), BF16),
        compiler_params=_params("parallel", "arbitrary"),
        name="mla_prompt",
    )(q, k, v)


def _mla_sample_kernel(n_new, n_heads, v_head, pt_ref, ql_ref, qp_ref, cn_ref, kn_ref, snt_ref, wuv_ref,
                       ckv_hbm, kpe_hbm, ksc_hbm, o_ref, cbuf, pbuf, sbuf, sems, m_ref, l_ref, acc_ref):
    n_seq, n_pages = pt_ref.shape
    pages = cbuf.shape[1]
    page_size = cbuf.shape[2]
    chunks_per_seq = n_pages // pages
    n_chunks = n_seq * chunks_per_seq
    rows = ql_ref.shape[1]
    tok = pages * page_size
    b = pl.program_id(0)
    c = pl.program_id(1)
    t = b * chunks_per_seq + c
    slot = t % 2

    def copies(t, slot):
        b = t // chunks_per_seq
        c = t % chunks_per_seq
        out = []
        for r in range(pages):
            page = pt_ref[b, c * pages + r]
            out.append(pltpu.make_async_copy(ckv_hbm.at[page], cbuf.at[slot, r], sems.at[0, slot]))
            out.append(pltpu.make_async_copy(kpe_hbm.at[page], pbuf.at[slot, r], sems.at[1, slot]))
            out.append(pltpu.make_async_copy(ksc_hbm.at[page], sbuf.at[slot, r], sems.at[2, slot]))
        return out

    def online_update(s, v_bf):
        m = m_ref[...]
        m_new = jnp.maximum(m, jnp.max(s, axis=-1, keepdims=True))
        p = jnp.exp2(s - m_new)
        alpha = jnp.exp2(m - m_new)
        l_ref[...] = alpha * l_ref[...] + jnp.sum(p, axis=-1, keepdims=True)
        acc_ref[...] = alpha * acc_ref[...] + jnp.dot(p.astype(BF16), v_bf, preferred_element_type=F32)
        m_ref[...] = m_new

    @pl.when(t == 0)
    def _():
        for cp in copies(0, 0):
            cp.start()

    @pl.when(t + 1 < n_chunks)
    def _():
        for cp in copies(t + 1, 1 - slot):
            cp.start()

    for cp in copies(t, slot):
        cp.wait()

    @pl.when(c == 0)
    def _():
        m_ref[...] = jnp.full(m_ref.shape, NEG_INF, F32)
        l_ref[...] = jnp.zeros(l_ref.shape, F32)
        acc_ref[...] = jnp.zeros(acc_ref.shape, F32)

    ql = ql_ref[0]
    qp = qp_ref[0]
    c_bf = cbuf[slot].reshape(tok, cbuf.shape[3]).astype(BF16)
    kp_t = jnp.concatenate([pbuf[slot, r] for r in range(pages)], axis=1)
    ks_t = jnp.concatenate([sbuf[slot, r] for r in range(pages)], axis=1)
    s = _bdot_nt(ql, c_bf) + _bdot(qp, kp_t)
    s = s * jnp.concatenate([ks_t] * n_new, axis=0)
    online_update(s, c_bf)

    @pl.when(c == chunks_per_seq - 1)
    def _():
        cn = cn_ref[0].astype(BF16).astype(F32)
        kn = kn_ref[0].astype(BF16).astype(F32)
        snt = snt_ref[0]
        qlf = ql.astype(F32)
        qpf = qp.astype(F32)
        q_tok = lax.broadcasted_iota(jnp.int32, (rows, 1), 0) // n_heads
        m = m_ref[...]
        l = l_ref[...]
        acc = acc_ref[...]
        for tk in range(n_new):
            st = (jnp.sum(qlf * cn[tk:tk + 1, :], axis=-1, keepdims=True)
                  + jnp.sum(qpf * kn[tk:tk + 1, :], axis=-1, keepdims=True)) * snt[:, tk:tk + 1]
            vis = q_tok >= tk
            m_new = jnp.where(vis, jnp.maximum(m, st), m)
            p = jnp.where(vis, jnp.exp2(st - m_new), 0.0)
            alpha = jnp.exp2(m - m_new)
            l = alpha * l + p
            acc = alpha * acc + p.astype(BF16).astype(F32) * cn[tk:tk + 1, :]
            m = m_new
        o_lat = (acc / l).astype(BF16)
        full = jnp.dot(o_lat, wuv_ref[...], preferred_element_type=F32)
        r_head = lax.broadcasted_iota(jnp.int32, full.shape, 0) % n_heads
        c_head = lax.broadcasted_iota(jnp.int32, full.shape, 1) // v_head
        full = jnp.where(r_head == c_head, full, 0.0)
        o_ref[0] = jnp.sum(full.reshape(n_new, n_heads, full.shape[1]), axis=1).astype(BF16)


def _mla_sample(page_table, ql, qp, cn, kn, snt, w_uv_flat, ckv, kpe, ksc, n_new, n_heads, v_head):
    n_seq, rows = ql.shape[:2]
    page_size = ckv.shape[1]
    pages = min(PAGES_PER_CHUNK, page_table.shape[1])
    per_seq = lambda a: pl.BlockSpec((1,) + a.shape[1:], lambda b, c, pt: (b, 0, 0))
    grid_spec = pltpu.PrefetchScalarGridSpec(
        num_scalar_prefetch=1,
        grid=(n_seq, page_table.shape[1] // pages),
        in_specs=[per_seq(ql), per_seq(qp), per_seq(cn), per_seq(kn), per_seq(snt),
                  pl.BlockSpec(w_uv_flat.shape, lambda b, c, pt: (0, 0)),
                  pl.BlockSpec(memory_space=pl.ANY), pl.BlockSpec(memory_space=pl.ANY),
                  pl.BlockSpec(memory_space=pl.ANY)],
        out_specs=pl.BlockSpec((1, n_new, n_heads * v_head), lambda b, c, pt: (b, 0, 0)),
        scratch_shapes=[pltpu.VMEM((2, pages, page_size, ckv.shape[2]), F32),
                        pltpu.VMEM((2, pages, kpe.shape[1], page_size), F32),
                        pltpu.VMEM((2, pages, ksc.shape[1], page_size), F32),
                        pltpu.SemaphoreType.DMA((3, 2)),
                        pltpu.VMEM((rows, 1), F32), pltpu.VMEM((rows, 1), F32),
                        pltpu.VMEM((rows, ckv.shape[2]), F32)])
    return pl.pallas_call(
        functools.partial(_mla_sample_kernel, n_new, n_heads, v_head),
        grid_spec=grid_spec,
        out_shape=jax.ShapeDtypeStruct((n_seq, n_new, n_heads * v_head), BF16),
        compiler_params=_params("arbitrary", "arbitrary"),
        name="mla_sample",
    )(page_table, ql, qp, cn, kn, snt, w_uv_flat, ckv, kpe, ksc)


def _mem_kv_kernel(mem_heads, mem_ref, g_ref, wk_ref, wv_ref, gk_ref, k_ref, v_ref):
    m = _rms(mem_ref[...], g_ref[...]).astype(BF16)
    k = jnp.dot(m, wk_ref[...], preferred_element_type=F32)
    d = k.shape[1] // mem_heads
    for hd in range(mem_heads):
        k_ref[:, hd * d:(hd + 1) * d] = _rms(k[:, hd * d:(hd + 1) * d], gk_ref[...])
    v_ref[...] = jnp.dot(m, wv_ref[...], preferred_element_type=F32)


def _mem_kv(mem, w, mem_heads):
    n, d = mem.shape
    ins = [mem, w["gmem"], w["w_mk"], w["w_mv"], w["gmk"]]
    return pl.pallas_call(
        functools.partial(_mem_kv_kernel, mem_heads),
        grid=(1,),
        in_specs=[_const_spec(a.shape) for a in ins],
        out_specs=[_whole_out_spec((n, d))] * 2,
        out_shape=[jax.ShapeDtypeStruct((n, d), F32)] * 2,
        compiler_params=_params("arbitrary"),
        name="mem_kv",
    )(*ins)


def _mem_attend_kernel(mem_heads, q_ref, k_ref, v_ref, o_ref):
    q = q_ref[0]
    k = k_ref[0].astype(BF16)
    v = v_ref[0].astype(BF16)
    d = q.shape[1] // mem_heads
    for hd in range(mem_heads):
        sl = slice(hd * d, (hd + 1) * d)
        s = _bdot_nt(q[:, sl], k[:, sl])
        p = jnp.exp(s - jnp.max(s, axis=-1, keepdims=True))
        p = p / jnp.sum(p, axis=-1, keepdims=True)
        o_ref[0, :, sl] = jnp.dot(p.astype(BF16), v[:, sl], preferred_element_type=F32).astype(BF16)


def _mem_attend(q, k, v, mem_heads, shared_kv):
    g, tq, d = q.shape
    n_mem = k.shape[1]
    kv_map = (lambda i: (0, 0, 0)) if shared_kv else (lambda i: (i, 0, 0))
    return pl.pallas_call(
        functools.partial(_mem_attend_kernel, mem_heads),
        grid=(g,),
        in_specs=[pl.BlockSpec((1, tq, d), lambda i: (i, 0, 0)),
                  pl.BlockSpec((1, n_mem, d), kv_map), pl.BlockSpec((1, n_mem, d), kv_map)],
        out_specs=pl.BlockSpec((1, tq, d), lambda i: (i, 0, 0)),
        out_shape=jax.ShapeDtypeStruct((g, tq, d), BF16),
        compiler_params=_params("parallel"),
        name="mem_attend_prompt" if shared_kv else "mem_attend_sample",
    )(q, k, v)


def _merge_kernel(x_ref, a_ref, mla_ref, mem_ref, g1_ref, wg_ref, wo_ref, wmo_ref, wout_ref, h_ref):
    x = x_ref[...]
    d = x.shape[1]
    gates = jax.nn.sigmoid(jnp.dot(_rms(x, g1_ref[...]).astype(BF16), wg_ref[...], preferred_element_type=F32))
    br_b = jnp.dot(mla_ref[...], wo_ref[...], preferred_element_type=F32)
    br_c = jnp.dot(mem_ref[...], wmo_ref[...], preferred_element_type=F32)
    mix = gates[:, :d] * a_ref[...] + gates[:, d:2 * d] * br_b + gates[:, 2 * d:] * br_c
    h_ref[...] = x + jnp.dot(mix.astype(BF16), wout_ref[...], preferred_element_type=F32)


def _merge(x, a, mla_o, mem_o, w, w_o, tm):
    n, d = x.shape
    ins = [x, a, mla_o, mem_o, w["g1"], w["w_gate"], w_o, w["w_mo"], w["w_out"]]
    return pl.pallas_call(
        _merge_kernel,
        grid=(n // tm,),
        in_specs=[_row_spec(tm, d), _row_spec(tm, d), _row_spec(tm, mla_o.shape[1]), _row_spec(tm, d)]
        + [_const_spec(a_.shape) for a_ in ins[4:]],
        out_specs=_row_spec(tm, d),
        out_shape=jax.ShapeDtypeStruct((n, d), F32),
        compiler_params=_params("parallel"),
        name="merge",
    )(*ins)


FFN_HALO = 8
FFN_CHUNK = 64


def _ffn_prompt_kernel(conv_w, x_ref, g2_ref, wup_ref, cw_ref, cb_ref, wdown_ref, y_ref, tail_ref, ext_ref, act_ref):
    tm = x_ref.shape[0]
    d_ff = wdown_ref.shape[0]

    @pl.when(pl.program_id(0) == 0)
    def _():
        ext_ref[0:FFN_HALO, :] = jnp.zeros((FFN_HALO, ext_ref.shape[1]), F32)

    x = x_ref[...]
    ext_ref[FFN_HALO:, :] = jnp.dot(_rms(x, g2_ref[...]).astype(BF16), wup_ref[...], preferred_element_type=F32)
    first = FFN_HALO - (conv_w - 1)
    for r0 in range(0, tm, FFN_CHUNK):
        hc = jnp.broadcast_to(cb_ref[...], (FFN_CHUNK, ext_ref.shape[1]))
        for k in range(conv_w):
            hc = hc + cw_ref[k:k + 1, :] * ext_ref[r0 + first + k:r0 + first + k + FFN_CHUNK, :]
        act_ref[r0:r0 + FFN_CHUNK, :] = (jax.nn.silu(hc[:, :d_ff]) * hc[:, d_ff:]).astype(BF16)
    y_ref[...] = x + jnp.dot(act_ref[...], wdown_ref[...], preferred_element_type=F32)
    tail = ext_ref[tm:tm + FFN_HALO, :]
    tail_ref[...] = tail
    ext_ref[0:FFN_HALO, :] = tail


def _ffn_prompt(x, w, tm):
    n, d = x.shape
    d_up = w["w_up"].shape[1]
    conv_w = w["ffn_conv_w"].shape[0]
    ins = [x, w["g2"], w["w_up"], w["ffn_conv_w"], w["ffn_conv_b"], w["w_down"]]
    return pl.pallas_call(
        functools.partial(_ffn_prompt_kernel, conv_w),
        grid=(n // tm,),
        in_specs=[_row_spec(tm, d)] + [_const_spec(a.shape) for a in ins[1:]],
        out_specs=[_row_spec(tm, d), _whole_out_spec((FFN_HALO, d_up))],
        out_shape=[jax.ShapeDtypeStruct((n, d), F32), jax.ShapeDtypeStruct((FFN_HALO, d_up), F32)],
        scratch_shapes=[pltpu.VMEM((tm + FFN_HALO, d_up), F32), pltpu.VMEM((tm, d_up // 2), BF16)],
        compiler_params=_params("arbitrary"),
        name="ffn_prompt",
    )(*ins)


def _ffn_sample_kernel(conv_w, n_new, x_ref, hist_ref, g2_ref, wup_ref, cw_ref, cb_ref, wdown_ref, y_ref, up_ref):
    n_seq = hist_ref.shape[1]
    d_ff = wdown_ref.shape[0]
    x = x_ref[...]
    up_ref[...] = jnp.dot(_rms(x, g2_ref[...]).astype(BF16), wup_ref[...], preferred_element_type=F32)

    def ext(j):
        if j < conv_w - 1:
            return hist_ref[j]
        j -= conv_w - 1
        return up_ref[j * n_seq:(j + 1) * n_seq, :]

    for s in range(n_new):
        hc = jnp.broadcast_to(cb_ref[...], (n_seq, up_ref.shape[1]))
        for k in range(conv_w):
            hc = hc + cw_ref[k:k + 1, :] * ext(s + k)
        act = (jax.nn.silu(hc[:, :d_ff]) * hc[:, d_ff:]).astype(BF16)
        rows = slice(s * n_seq, (s + 1) * n_seq)
        y_ref[rows, :] = x[rows, :] + jnp.dot(act, wdown_ref[...], preferred_element_type=F32)


def _ffn_sample(x, hist_t, w, n_new):
    n, d = x.shape
    d_up = w["w_up"].shape[1]
    conv_w = w["ffn_conv_w"].shape[0]
    ins = [x, hist_t, w["g2"], w["w_up"], w["ffn_conv_w"], w["ffn_conv_b"], w["w_down"]]
    return pl.pallas_call(
        functools.partial(_ffn_sample_kernel, conv_w, n_new),
        grid=(1,),
        in_specs=[_const_spec(a.shape) for a in ins],
        out_specs=[_whole_out_spec((n, d)), _whole_out_spec((n, d_up))],
        out_shape=[jax.ShapeDtypeStruct((n, d), F32), jax.ShapeDtypeStruct((n, d_up), F32)],
        compiler_params=_params("arbitrary"),
        name="ffn_sample",
    )(*ins)


def _pad_heads(w3, width):
    k, h, d = w3.shape
    return jnp.concatenate([w3, jnp.zeros((k, h, width - d), w3.dtype)], axis=-1).reshape(k, h * width)


def _rot_half(w, half):
    return jnp.concatenate([-w[..., half:], w[..., :half]], axis=-1)


def _swap_half(g, half):
    return jnp.concatenate([g[..., half:], g[..., :half]], axis=-1)


def _lane_place(v, offset):
    k, d = v.shape
    return jnp.concatenate([jnp.zeros((k, offset), v.dtype), v, jnp.zeros((k, HEAD_PAD - offset - d), v.dtype)], axis=-1)


def _prep_weights(norm1_g, w_in, q_a_norm_g, w_uq, kv_a_norm_g, w_uk, w_uv, q_norm_g, k_norm_g, w_o_mla, conv_w, conv_b,
                  conv_ln_g, conv_ln_b, w_conv_out, mem_norm_g, w_mk, w_mv, mq_norm_g, mk_norm_g, w_mo, w_out, norm2_g,
                  w_up, ffn_conv_w, ffn_conv_b, w_down, rope, mem_heads):
    hp = HEAD_PAD
    d = w_in.shape[0]
    c_conv = conv_w.shape[1]
    q_lora = w_uq.shape[0]
    kv_lora, n_heads, nope = w_uk.shape
    v_head = w_uv.shape[2]
    half = rope // 2
    o_qa = 2 * c_conv
    o_kva = o_qa + q_lora
    o_mq = o_kva + kv_lora + rope
    o_gate = o_mq + w_mk.shape[1]
    w_pe = w_in[:, o_kva + kv_lora:o_mq]
    w_main = jnp.concatenate([w_in[:, :o_kva + kv_lora], _lane_place(w_pe, nope), _lane_place(_rot_half(w_pe, half), nope),
                              w_in[:, o_mq:o_gate]], axis=1).astype(BF16)
    uq3 = w_uq.reshape(q_lora, n_heads, nope + rope)
    uq_rot3 = jnp.concatenate([jnp.zeros((q_lora, n_heads, nope), F32), _rot_half(uq3[..., nope:], half)], axis=-1)
    row = lambda g: g.reshape(1, -1)
    sel = (jnp.arange(n_heads * hp)[:, None]
           == ((jnp.arange(n_heads * rope) // rope) * hp + nope + jnp.arange(n_heads * rope) % rope)[None, :])
    w_abs = jnp.transpose(w_uk, (1, 2, 0))
    w_abs = jnp.concatenate([w_abs, jnp.zeros((n_heads, hp - nope, kv_lora), F32)], axis=1)
    w_o3 = w_o_mla.reshape(n_heads, v_head, d)
    w_o_pad = jnp.concatenate([w_o3, jnp.zeros((n_heads, hp - v_head, d), F32)], axis=1).reshape(n_heads * hp, d)
    return {
        "g1": row(norm1_g), "w_main": w_main, "w_gate": w_in[:, o_gate:].astype(BF16),
        "gqa": row(q_a_norm_g), "w_uq_pad": _pad_heads(uq3, hp).astype(BF16), "w_uq_rot": _pad_heads(uq_rot3, hp).astype(BF16),
        "gq_pad": _lane_place(row(q_norm_g), 0),
        "gq_rot": _lane_place(_swap_half(row(q_norm_g)[:, nope:], half), nope),
        "gkv": row(kv_a_norm_g), "w_uk_pad": _pad_heads(w_uk, hp).astype(BF16), "w_uv_pad": _pad_heads(w_uv, hp).astype(BF16),
        "gkn_pad": _lane_place(row(k_norm_g)[:, :nope], 0),
        "gkp_pad": _lane_place(row(k_norm_g)[:, nope:], nope),
        "gkp_rot": _lane_place(_swap_half(row(k_norm_g)[:, nope:], half), nope),
        "gmq": row(mq_norm_g), "w_abs": w_abs.astype(BF16), "sel_pe": sel.astype(BF16),
        "w_uv_flat": w_uv.reshape(kv_lora, n_heads * v_head).astype(BF16),
        "w_o_pad": w_o_pad.astype(BF16), "w_o": w_o_mla.astype(BF16),
        "conv_w": conv_w, "conv_b": row(conv_b), "conv_ln_g": row(conv_ln_g), "conv_ln_b": row(conv_ln_b),
        "conv_w8": jnp.broadcast_to(conv_w[:, None, :], (conv_w.shape[0], SUBLANES, c_conv)),
        "conv_b8": jnp.broadcast_to(row(conv_b), (SUBLANES, c_conv)),
        "w_conv_out": w_conv_out.astype(BF16),
        "gmem": row(mem_norm_g), "w_mk": w_mk.astype(BF16), "w_mv": w_mv.astype(BF16), "gmk": row(mk_norm_g),
        "w_mo": w_mo.astype(BF16), "w_out": w_out.astype(BF16),
        "g2": row(norm2_g), "w_up": w_up.astype(BF16), "ffn_conv_w": ffn_conv_w, "ffn_conv_b": row(ffn_conv_b),
        "w_down": w_down.astype(BF16),
    }


def _rope_tables(pos, rope, nope):
    half = rope // 2
    inv_freq = ROPE_THETA ** (-jnp.arange(half, dtype=F32) / half)
    ang = pos.astype(F32)[:, None] * inv_freq[None, :]
    cos, sin = jnp.cos(ang), jnp.sin(ang)
    n = pos.shape[0]
    pad = jnp.zeros((n, HEAD_PAD - nope - rope), F32)
    cos_t = jnp.concatenate([jnp.ones((n, nope), F32), cos, cos, pad], axis=-1)
    sin_t = jnp.concatenate([jnp.zeros((n, nope), F32), sin, sin, pad], axis=-1)
    return cos_t, sin_t


def kernel(x_prompt, x_sample, mem_prompt, cache_ckv, cache_kpe, cache_kscale, page_table, state_conv, state_ffn, cache_mem_k, cache_mem_v, norm1_g, w_in, q_a_norm_g, w_uq, kv_a_norm_g, w_uk, w_uv, q_norm_g, k_norm_g, w_o_mla, conv_w, conv_b, conv_ln_g, conv_ln_b, w_conv_out, mem_norm_g, w_mk, w_mv, mq_norm_g, mk_norm_g, w_mo, w_out, norm2_g, w_up, ffn_conv_w, ffn_conv_b, w_down):
    depth = w_in.shape[0]
    assert depth == 1, "single trunk layer"
    b_p, s_p, d = x_prompt.shape
    assert b_p == 1, "one prompt sequence"
    n_seq, n_new, _ = x_sample.shape
    page_size = cache_ckv.shape[2]
    n_past = page_table.shape[1] * page_size
    kv_lora, n_heads, nope = w_uk.shape[1:]
    rope = cache_kpe.shape[-1]
    v_head = w_uv.shape[-1]
    qk_dim = nope + rope
    mem_heads, mem_dim = cache_mem_k.shape[-2:]
    n_mem = mem_prompt.shape[1]
    c_conv = conv_w.shape[-1]
    conv_width = conv_w.shape[1]
    ffn_width = ffn_conv_w.shape[1]
    d_up = w_up.shape[-1]
    hp = HEAD_PAD
    dims = (c_conv, w_uq.shape[1], kv_lora, n_heads, nope, rope, v_head, mem_heads, mem_dim)

    w = _prep_weights(norm1_g[0], w_in[0], q_a_norm_g[0], w_uq[0], kv_a_norm_g[0], w_uk[0], w_uv[0], q_norm_g[0],
                      k_norm_g[0], w_o_mla[0], conv_w[0], conv_b[0], conv_ln_g[0], conv_ln_b[0], w_conv_out[0],
                      mem_norm_g[0], w_mk[0], w_mv[0], mq_norm_g[0], mk_norm_g[0], w_mo[0], w_out[0], norm2_g[0],
                      w_up[0], ffn_conv_w[0], ffn_conv_b[0], w_down[0], rope, mem_heads)

    tm = min(ROW_TILE, s_p)
    xp = x_prompt.reshape(s_p, d)
    cos_p, sin_p = _rope_tables(jnp.arange(s_p), rope, nope)
    u_p, ckv_p, kpe_p, ksc_p, mq_p, q_p, k_p, v_p = _front_end(xp, cos_p, sin_p, w, dims, False, tm)
    a_p = _conv_prompt(u_p, w, tm)
    mla_p = _mla_prompt(q_p, k_p, v_p, n_heads, v_head, min(ATTN_Q_TILE, s_p), min(ATTN_KV_TILE, s_p))
    mk_p, mv_p = _mem_kv(mem_prompt.reshape(n_mem, d), w, mem_heads)
    mem_o_p = _mem_attend(mq_p.reshape(s_p // tm, tm, d), mk_p[None], mv_p[None], mem_heads, True).reshape(s_p, d)
    h_p = _merge(xp, a_p, mla_p, mem_o_p, w, w["w_o_pad"], tm)
    y_p, ffn_tail = _ffn_prompt(h_p, w, tm)

    n_s = n_seq * n_new
    xs = jnp.transpose(x_sample, (1, 0, 2)).reshape(n_s, d)
    pos_s = jnp.repeat(n_past + jnp.arange(n_new), n_seq)
    cos_s, sin_s = _rope_tables(pos_s, rope, nope)
    tms = min(ROW_TILE, n_s)
    u_s, ckv_s, kpe_s, ksc_s, mq_s, qlat_s, qpe_s, _ = _front_end(xs, cos_s, sin_s, w, dims, True, tms)

    def seq_major(a):
        return jnp.transpose(a.reshape(n_new, n_seq, -1), (1, 0, 2))

    ext_t = jnp.concatenate([jnp.transpose(state_conv[0], (1, 0, 2)), u_s.reshape(n_new, n_seq, c_conv)], axis=0)
    a_s = _conv_sample(ext_t, w, n_new, min(32, n_seq)).reshape(n_s, d)

    ckv_s_b = seq_major(ckv_s)
    kpe_s_b = seq_major(kpe_s)[..., nope:nope + rope]
    ksc_s_b = seq_major(ksc_s)[..., :n_heads]
    pad_rows = (-n_new) % 8
    pad_new = lambda a: jnp.pad(a, ((0, 0), (0, pad_rows), (0, 0)))
    snt = jnp.tile(jnp.transpose(ksc_s_b, (0, 2, 1)), (1, n_new, 1))
    snt = jnp.pad(snt, ((0, 0), (0, 0), (0, pad_rows)))
    ql = seq_major(qlat_s).reshape(n_seq, n_new * n_heads, kv_lora)
    qp = seq_major(qpe_s).reshape(n_seq, n_new * n_heads, rope)
    mla_s = _mla_sample(page_table, ql, qp, pad_new(ckv_s_b), pad_new(kpe_s_b), snt, w["w_uv_flat"],
                        cache_ckv.reshape(cache_ckv.shape[1:]),
                        jnp.swapaxes(cache_kpe.reshape(cache_kpe.shape[1:]), 1, 2),
                        jnp.swapaxes(cache_kscale.reshape(cache_kscale.shape[1:]), 1, 2), n_new, n_heads, v_head)
    mla_s_t = jnp.transpose(mla_s, (1, 0, 2)).reshape(n_s, n_heads * v_head)

    mem_o_s = _mem_attend(seq_major(mq_s), cache_mem_k.reshape(n_seq, n_mem, d), cache_mem_v.reshape(n_seq, n_mem, d),
                          mem_heads, False)
    mem_o_s_t = jnp.transpose(mem_o_s, (1, 0, 2)).reshape(n_s, d)
    h_s = _merge(xs, a_s, mla_s_t, mem_o_s_t, w, w["w_o"], tms)
    y_s, up_s = _ffn_sample(h_s, jnp.transpose(state_ffn[0], (1, 0, 2)), w, n_new)

    p_ckv = ckv_p.reshape(1, 1, s_p, kv_lora)
    p_kpe = kpe_p[:, nope:nope + rope].reshape(1, 1, s_p, rope)
    p_ksc = ksc_p[:, :n_heads].reshape(1, 1, s_p, n_heads)
    p_conv = u_p[s_p - (conv_width - 1):].reshape(1, 1, conv_width - 1, c_conv)
    p_ffn = ffn_tail[FFN_HALO - (ffn_width - 1):].reshape(1, 1, ffn_width - 1, d_up)
    p_mem_k = mk_p.reshape(1, 1, n_mem, mem_heads, mem_dim)
    p_mem_v = mv_p.reshape(1, 1, n_mem, mem_heads, mem_dim)
    s_conv = jnp.transpose(ext_t[n_new:], (1, 0, 2))[None]
    ffn_ext = jnp.concatenate([state_ffn[0], seq_major(up_s)], axis=1)
    s_ffn = ffn_ext[:, n_new:][None]
    return (y_p.reshape(1, s_p, d), seq_major(y_s), p_ckv, p_kpe, p_ksc, p_conv, p_ffn, p_mem_k, p_mem_v,
            ckv_s_b[None], kpe_s_b[None], ksc_s_b[None], s_conv, s_ffn)
```

```python
import functools

import jax
import jax.numpy as jnp
from jax import lax
from jax.experimental import pallas as pl
from jax.experimental.pallas import tpu as pltpu

F32 = jnp.float32
BF16 = jnp.bfloat16
EPS = 1e-6
ROPE_THETA = 10000.0
HEAD_PAD = 128
VMEM_LIMIT = 56 * 1024 * 1024
ROW_TILE = 512
ATTN_Q_TILE = 1024
ATTN_KV_TILE = 1024
ATTN_DIAG_STRIPS = 2
PAGES_PER_CHUNK = 32
SAMPLE_SUBCHUNKS = 4
NEG_INF = float("-inf")
LOG2E = 1.4426950408889634


def _const_spec(shape):
    nd = len(shape)
    return pl.BlockSpec(shape, lambda *_: (0,) * nd, pipeline_mode=pl.Buffered(1))


def _whole_out_spec(shape):
    nd = len(shape)
    return pl.BlockSpec(shape, lambda *_: (0,) * nd)


def _row_spec(tm, width):
    return pl.BlockSpec((tm, width), lambda i: (i, 0))


def _params(*sem):
    return pltpu.CompilerParams(dimension_semantics=sem, vmem_limit_bytes=VMEM_LIMIT)


def _bdot(a, b):
    return jnp.dot(a.astype(BF16), b.astype(BF16), preferred_element_type=F32)


def _bdot_nt(a, b):
    return lax.dot_general(a.astype(BF16), b.astype(BF16), (((1,), (1,)), ((), ())), preferred_element_type=F32)


def _rms(x, g):
    return x * lax.rsqrt(jnp.mean(x * x, axis=-1, keepdims=True) + EPS) * g


def _front_end_kernel(dims, sample, x_ref, cos_ref, sin_ref, g1_ref, wmain_ref, gqa_ref, wuq_ref, wuqr_ref,
                      gq_ref, gqr_ref, gkv_ref, wuk_ref, gkn_ref, gkp_ref, gkpr_ref, gmq_ref, *rest):
    c_conv, q_lora, kv_lora, n_heads, nope, rope, v_head, mem_heads, mem_dim = dims
    qk_dim = nope + rope
    sm_scale = qk_dim ** -0.5 * LOG2E
    if sample:
        wabs_ref, sel_ref, u_ref, ckv_ref, kpe_ref, ksc_ref, mq_ref, qlat_ref, qpe_ref, qs_ref = rest
    else:
        wuv_ref, u_ref, ckv_ref, kpe_ref, ksc_ref, mq_ref, q_ref, k_ref, v_ref = rest
    hp = HEAD_PAD
    h = _rms(x_ref[...], g1_ref[...]).astype(BF16)
    z = jnp.dot(h, wmain_ref[...], preferred_element_type=F32)
    o_qa = 2 * c_conv
    o_c = o_qa + q_lora
    o_pe = o_c + kv_lora
    o_per = o_pe + hp
    o_mq = o_per + hp
    u_ref[...] = z[:, :c_conv] * jax.nn.sigmoid(z[:, c_conv:o_qa])

    cos_t = cos_ref[...]
    sin_t = sin_ref[...]
    qa = _rms(z[:, o_qa:o_c], gqa_ref[...]).astype(BF16)
    q = jnp.dot(qa, wuq_ref[...], preferred_element_type=F32)
    qr = jnp.dot(qa, wuqr_ref[...], preferred_element_type=F32)
    g_cos = gq_ref[...] * cos_t
    g_sin = gqr_ref[...] * sin_t
    q_out = qs_ref if sample else q_ref
    for hd in range(n_heads):
        sl = slice(hd * hp, (hd + 1) * hp)
        qh = q[:, sl]
        r = lax.rsqrt(jnp.sum(qh * qh, axis=-1, keepdims=True) / qk_dim + EPS)
        qn = r * (qh * g_cos + qr[:, sl] * g_sin) * sm_scale
        q_out[:, sl] = qn.astype(BF16)
        if sample:
            qlat_ref[:, hd * kv_lora:(hd + 1) * kv_lora] = _bdot(qn * gkn_ref[...], wabs_ref[hd]).astype(BF16)
    if sample:
        qpe_ref[...] = jnp.dot(qs_ref[...], sel_ref[...], preferred_element_type=F32).astype(BF16)

    c_kv = _rms(z[:, o_c:o_pe], gkv_ref[...])
    ckv_ref[...] = c_kv
    pe = z[:, o_pe:o_per]
    k_pe = pe * (gkp_ref[...] * cos_t) + z[:, o_per:o_mq] * (gkpr_ref[...] * sin_t)
    kpe_ref[...] = k_pe
    ss_pe = jnp.sum(pe * pe, axis=-1, keepdims=True)
    c_bf = c_kv.astype(BF16)
    k_nope = jnp.dot(c_bf, wuk_ref[...], preferred_element_type=F32)
    lane = lax.broadcasted_iota(jnp.int32, (x_ref.shape[0], hp), 1)
    ksc_all = jnp.zeros((x_ref.shape[0], hp), F32)
    for hd in range(n_heads):
        sl = slice(hd * hp, (hd + 1) * hp)
        kn = k_nope[:, sl]
        ksc = lax.rsqrt((jnp.sum(kn * kn, axis=-1, keepdims=True) + ss_pe) / qk_dim + EPS)
        ksc_all = jnp.where(lane == hd, ksc, ksc_all)
        if not sample:
            k_ref[:, sl] = ((kn * gkn_ref[...] + k_pe) * ksc).astype(BF16)
    ksc_ref[...] = ksc_all
    if not sample:
        v = jnp.dot(c_bf, wuv_ref[...], preferred_element_type=F32)
        head_lane = lax.broadcasted_iota(jnp.int32, v.shape, 1) % hp
        v_ref[...] = jnp.where(head_lane == v_head, 1.0, v).astype(BF16)

    for hd in range(mem_heads):
        sl = slice(hd * mem_dim, (hd + 1) * mem_dim)
        mq_ref[:, sl] = (_rms(z[:, o_mq + hd * mem_dim:o_mq + (hd + 1) * mem_dim], gmq_ref[...])
                         * (mem_dim ** -0.5)).astype(BF16)


def _front_end(x, cos_t, sin_t, w, dims, sample, tm):
    n, d = x.shape
    c_conv, q_lora, kv_lora, n_heads, nope, rope, v_head, mem_heads, mem_dim = dims
    hp = HEAD_PAD
    ins =[x, cos_t, sin_t, w["g1"], w["w_main"], w["gqa"], w["w_uq_pad"], w["w_uq_rot"], w["gq_pad"], w["gq_rot"],
           w["gkv"], w["w_uk_pad"], w["gkn_pad"], w["gkp_pad"], w["gkp_rot"], w["gmq"]]
    ins += [w["w_abs"], w["sel_pe"]] if sample else [w["w_uv_pad"]]
    in_specs = [_row_spec(tm, d), _row_spec(tm, hp), _row_spec(tm, hp)] + [_const_spec(a.shape) for a in ins[3:]]
    outs = [(c_conv, F32), (kv_lora, F32), (hp, F32), (hp, F32), (mem_heads * mem_dim, BF16)]
    if sample:
        outs += [(n_heads * kv_lora, BF16), (n_heads * rope, BF16), (n_heads * hp, BF16)]
    else:
        outs += [(n_heads * hp, BF16)] * 3
    return pl.pallas_call(
        functools.partial(_front_end_kernel, dims, sample),
        grid=(n // tm,),
        in_specs=in_specs,
        out_specs=[_row_spec(tm, wd) for wd, _ in outs],
        out_shape=[jax.ShapeDtypeStruct((n, wd), dt) for wd, dt in outs],
        compiler_params=_params("parallel"),
        name="front_end_sample" if sample else "front_end_prompt",
    )(*ins)


CONV_HALO = 32
CONV_CHUNK = 16
SUBLANES = 8


def _conv_prompt_kernel(conv_w, cur_ref, prev_ref, w_ref, b_ref, lng_ref, lnb_ref, wout_ref, a_ref, ext_ref, sh_ref,
                        act_ref):
    tm = cur_ref.shape[0]
    prev = prev_ref[...]
    ext_ref[0:CONV_HALO, :] = jnp.where(pl.program_id(0) == 0, jnp.zeros_like(prev), prev)
    ext_ref[CONV_HALO:, :] = cur_ref[...]
    span = sh_ref.shape[1]
    for phase in range(1, SUBLANES):
        sh_ref[phase - 1] = ext_ref[phase:phase + span, :]
    first = CONV_HALO - (conv_w - 1)
    groups = CONV_CHUNK // SUBLANES
    for r0 in range(0, tm, CONV_CHUNK):
        accs = [b_ref[...]] * groups
        for k in range(conv_w):
            phase = (first + k) % SUBLANES
            base = r0 + first + k - phase
            w8 = w_ref[k]
            for g in range(groups):
                lo = base + g * SUBLANES
                tap = ext_ref[lo:lo + SUBLANES, :] if phase == 0 else sh_ref[phase - 1, lo:lo + SUBLANES, :]
                accs[g] = accs[g] + w8 * tap
        acc = jnp.concatenate(accs, axis=0)
        mu = jnp.mean(acc, axis=-1, keepdims=True)
        cen = acc - mu
        var = jnp.mean(cen * cen, axis=-1, keepdims=True)
        y = cen * lax.rsqrt(var + EPS) * lng_ref[...] + lnb_ref[...]
        act_ref[r0:r0 + CONV_CHUNK, :] = jax.nn.silu(y).astype(BF16)
    a_ref[...] = jnp.dot(act_ref[...], wout_ref[...], preferred_element_type=F32)


def _conv_prompt(u, w, tm):
    n, c = u.shape
    conv_w = w["conv_w"].shape[0]
    ratio = tm // CONV_HALO
    return pl.pallas_call(
        functools.partial(_conv_prompt_kernel, conv_w),
        grid=(n // tm,),
        in_specs=[_row_spec(tm, c),
                  pl.BlockSpec((CONV_HALO, c), lambda i: (jnp.maximum(i * ratio - 1, 0), 0)),
                  _const_spec(w["conv_w8"].shape), _const_spec((SUBLANES, c)), _const_spec((1, c)), _const_spec((1, c)),
                  _const_spec(w["w_conv_out"].shape)],
        out_specs=_row_spec(tm, c),
        out_shape=jax.ShapeDtypeStruct((n, c), F32),
        scratch_shapes=[pltpu.VMEM((tm + CONV_HALO, c), F32),
                        pltpu.VMEM((SUBLANES - 1, tm + CONV_HALO - SUBLANES, c), F32),
                        pltpu.VMEM((tm, c), BF16)],
        compiler_params=_params("parallel"),
        name="conv_prompt",
    )(u, u, w["conv_w8"], w["conv_b8"], w["conv_ln_g"], w["conv_ln_b"], w["w_conv_out"])


def _conv_sample_kernel(conv_w, n_new, ext_ref, w_ref, b_ref, lng_ref, lnb_ref, wout_ref, a_ref):
    bb, c = ext_ref.shape[1], ext_ref.shape[2]
    for s in range(n_new):
        acc = jnp.broadcast_to(b_ref[...], (bb, c))
        for k in range(conv_w):
            acc = acc + w_ref[k:k + 1, :] * ext_ref[s + k]
        mu = jnp.mean(acc, axis=-1, keepdims=True)
        cen = acc - mu
        var = jnp.mean(cen * cen, axis=-1, keepdims=True)
        y = cen * lax.rsqrt(var + EPS) * lng_ref[...] + lnb_ref[...]
        a_ref[s] = jnp.dot(jax.nn.silu(y).astype(BF16), wout_ref[...], preferred_element_type=F32)


def _conv_sample(ext_t, w, n_new, bb):
    t, b, c = ext_t.shape
    conv_w = w["conv_w"].shape[0]
    return pl.pallas_call(
        functools.partial(_conv_sample_kernel, conv_w, n_new),
        grid=(b // bb,),
        in_specs=[pl.BlockSpec((t, bb, c), lambda i: (0, i, 0)),
                  _const_spec(w["conv_w"].shape), _const_spec((1, c)), _const_spec((1, c)), _const_spec((1, c)),
                  _const_spec(w["w_conv_out"].shape)],
        out_specs=pl.BlockSpec((n_new, bb, c), lambda i: (0, i, 0)),
        out_shape=jax.ShapeDtypeStruct((n_new, b, c), F32),
        compiler_params=_params("parallel"),
        name="conv_sample",
    )(ext_t, w["conv_w"], w["conv_b"], w["conv_ln_g"], w["conv_ln_b"], w["w_conv_out"])


def _mla_prompt_kernel(v_head, tk, q_ref, k_ref, v_ref, o_ref):
    tq = q_ref.shape[0]
    per_tile = tq // tk
    i = pl.program_id(1)
    q = q_ref[...]

    def update(start, width, r0, carry, masked):
        m, acc = carry
        s = _bdot_nt(q[r0:], k_ref[pl.ds(start, width), :])
        if masked:
            row = lax.broadcasted_iota(jnp.int32, s.shape, 0)
            col = lax.broadcasted_iota(jnp.int32, s.shape, 1)
            s = jnp.where(col <= row, s, NEG_INF)
        m_new = jnp.maximum(m[r0:], jnp.max(s, axis=-1, keepdims=True))
        p = jnp.exp2(s - m_new).astype(BF16)
        acc_new = jnp.exp2(m[r0:] - m_new) * acc[r0:] + jnp.dot(p, v_ref[pl.ds(start, width), :],
                                                                preferred_element_type=F32)
        if r0:
            m_new = jnp.concatenate([m[:r0], m_new], axis=0)
            acc_new = jnp.concatenate([acc[:r0], acc_new], axis=0)
        return m_new, acc_new

    carry = (jnp.full((tq, 1), NEG_INF, F32), jnp.zeros((tq, v_ref.shape[1]), F32))
    carry = lax.fori_loop(0, i * per_tile, lambda j, c: update(pl.multiple_of(j * tk, tk), tk, 0, c, False), carry)
    strip = tq // ATTN_DIAG_STRIPS
    for d in range(ATTN_DIAG_STRIPS):
        carry = update(pl.multiple_of(i * tq + d * strip, strip), strip, d * strip, carry, True)
    acc = carry[1]
    o_ref[...] = (acc / acc[:, v_head:v_head + 1]).astype(BF16)


def _mla_prompt(q, k, v, n_heads, v_head, tq, tk):
    n = q.shape[0]
    hp = HEAD_PAD
    return pl.pallas_call(
        functools.partial(_mla_prompt_kernel, v_head, tk),
        grid=(n_heads, n // tq),
        in_specs=[pl.BlockSpec((tq, hp), lambda h, i: (i, h)),
                  pl.BlockSpec((n, hp), lambda h, i: (0, h)),
                  pl.BlockSpec((n, hp), lambda h, i: (0, h))],
        out_specs=pl.BlockSpec((tq, hp), lambda h, i: (i, h)),
        out_shape=jax.ShapeDtypeStruct((n, n_heads * hp), BF16),
        compiler_params=_params("parallel", "arbitrary"),
        name="mla_prompt",
    )(q, k, v)


def _mla_sample_kernel(n_new, n_heads, v_head, pt_ref, ql_ref, qp_ref, cn_ref, kn_ref, snt_ref, wuv_ref,
                       ckv_hbm, kpe_hbm, ksc_hbm, o_ref, cbuf, pbuf, sbuf, sems, m_ref, l_ref, acc_ref):
    n_seq, n_pages = pt_ref.shape
    pages = cbuf.shape[1]
    page_size = cbuf.shape[2]
    chunks_per_seq = n_pages // pages
    n_chunks = n_seq * chunks_per_seq
    rows = ql_ref.shape[1]
    b = pl.program_id(0)
    c = pl.program_id(1)
    t = b * chunks_per_seq + c
    slot = t % 2

    def copies(t, slot):
        b = t // chunks_per_seq
        c = t % chunks_per_seq
        out = []
        for r in range(pages):
            page = pt_ref[b, c * pages + r]
            out.append(pltpu.make_async_copy(ckv_hbm.at[page], cbuf.at[slot, r], sems.at[0, slot]))
            out.append(pltpu.make_async_copy(kpe_hbm.at[page], pbuf.at[slot, r], sems.at[1, slot]))
            out.append(pltpu.make_async_copy(ksc_hbm.at[page], sbuf.at[slot, r], sems.at[2, slot]))
        return out

    def online_update(s, v_bf):
        m = m_ref[...]
        m_new = jnp.maximum(m, jnp.max(s, axis=-1, keepdims=True))
        p = jnp.exp2(s - m_new)
        alpha = jnp.exp2(m - m_new)
        l_ref[...] = alpha * l_ref[...] + jnp.sum(p, axis=-1, keepdims=True)
        acc_ref[...] = alpha * acc_ref[...] + jnp.dot(p.astype(BF16), v_bf, preferred_element_type=F32)
        m_ref[...] = m_new

    @pl.when(t == 0)
    def _():
        for cp in copies(0, 0):
            cp.start()

    @pl.when(t + 1 < n_chunks)
    def _():
        for cp in copies(t + 1, 1 - slot):
            cp.start()

    for cp in copies(t, slot):
        cp.wait()

    @pl.when(c == 0)
    def _():
        m_ref[...] = jnp.full(m_ref.shape, NEG_INF, F32)
        l_ref[...] = jnp.zeros(l_ref.shape, F32)
        acc_ref[...] = jnp.zeros(acc_ref.shape, F32)

    ql = ql_ref[0]
    qp = qp_ref[0]
    sub = pages // SAMPLE_SUBCHUNKS
    parts = []
    for i in range(SAMPLE_SUBCHUNKS):
        pr = range(i * sub, (i + 1) * sub)
        c_bf = cbuf[slot, i * sub:(i + 1) * sub].reshape(sub * page_size, cbuf.shape[3]).astype(BF16)
        kp_t = jnp.concatenate([pbuf[slot, r] for r in pr], axis=1)
        ks_t = jnp.concatenate([sbuf[slot, r] for r in pr], axis=1)
        s = _bdot_nt(ql, c_bf) + _bdot(qp, kp_t)
        parts.append((s * jnp.concatenate([ks_t] * n_new, axis=0), c_bf))
    for s, c_bf in parts:
        online_update(s, c_bf)

    @pl.when(c == chunks_per_seq - 1)
    def _():
        cn = cn_ref[0].astype(BF16).astype(F32)
        kn = kn_ref[0].astype(BF16).astype(F32)
        snt = snt_ref[0]
        qlf = ql.astype(F32)
        qpf = qp.astype(F32)
        q_tok = lax.broadcasted_iota(jnp.int32, (rows, 1), 0) // n_heads
        m = m_ref[...]
        l = l_ref[...]
        acc = acc_ref[...]
        for tk in range(n_new):
            st = (jnp.sum(qlf * cn[tk:tk + 1, :], axis=-1, keepdims=True)
                  + jnp.sum(qpf * kn[tk:tk + 1, :], axis=-1, keepdims=True)) * snt[:, tk:tk + 1]
            vis = q_tok >= tk
            m_new = jnp.where(vis, jnp.maximum(m, st), m)
            p = jnp.where(vis, jnp.exp2(st - m_new), 0.0)
            alpha = jnp.exp2(m - m_new)
            l = alpha * l + p
            acc = alpha * acc + p.astype(BF16).astype(F32) * cn[tk:tk + 1, :]
            m = m_new
        o_lat = (acc / l).astype(BF16)
        full = jnp.dot(o_lat, wuv_ref[...], preferred_element_type=F32)
        r_head = lax.broadcasted_iota(jnp.int32, full.shape, 0) % n_heads
        c_head = lax.broadcasted_iota(jnp.int32, full.shape, 1) // v_head
        full = jnp.where(r_head == c_head, full, 0.0)
        o_ref[0] = jnp.sum(full.reshape(n_new, n_heads, full.shape[1]), axis=1).astype(BF16)


def _mla_sample(page_table, ql, qp, cn, kn, snt, w_uv_flat, ckv, kpe, ksc, n_new, n_heads, v_head):
    n_seq, rows = ql.shape[:2]
    page_size = ckv.shape[1]
    pages = min(PAGES_PER_CHUNK, page_table.shape[1])
    assert page_table.shape[1] % pages == 0 and pages % SAMPLE_SUBCHUNKS == 0
    per_seq = lambda a: pl.BlockSpec((1,) + a.shape[1:], lambda b, c, pt: (b, 0, 0))
    grid_spec = pltpu.PrefetchScalarGridSpec(
        num_scalar_prefetch=1,
        grid=(n_seq, page_table.shape[1] // pages),
        in_specs=[per_seq(ql), per_seq(qp), per_seq(cn), per_seq(kn), per_seq(snt),
                  pl.BlockSpec(w_uv_flat.shape, lambda b, c, pt: (0, 0)),
                  pl.BlockSpec(memory_space=pl.ANY), pl.BlockSpec(memory_space=pl.ANY),
                  pl.BlockSpec(memory_space=pl.ANY)],
        out_specs=pl.BlockSpec((1, n_new, n_heads * v_head), lambda b, c, pt: (b, 0, 0)),
        scratch_shapes=[pltpu.VMEM((2, pages, page_size, ckv.shape[2]), F32),
                        pltpu.VMEM((2, pages, kpe.shape[1], page_size), F32),
                        pltpu.VMEM((2, pages, ksc.shape[1], page_size), F32),
                        pltpu.SemaphoreType.DMA((3, 2)),
                        pltpu.VMEM((rows, 1), F32), pltpu.VMEM((rows, 1), F32),
                        pltpu.VMEM((rows, ckv.shape[2]), F32)])
    return pl.pallas_call(
        functools.partial(_mla_sample_kernel, n_new, n_heads, v_head),
        grid_spec=grid_spec,
        out_shape=jax.ShapeDtypeStruct((n_seq, n_new, n_heads * v_head), BF16),
        compiler_params=_params("arbitrary", "arbitrary"),
        name="mla_sample",
    )(page_table, ql, qp, cn, kn, snt, w_uv_flat, ckv, kpe, ksc)


def _mem_kv_kernel(mem_heads, mem_ref, g_ref, wk_ref, wv_ref, gk_ref, k_ref, v_ref):
    m = _rms(mem_ref[...], g_ref[...]).astype(BF16)
    k = jnp.dot(m, wk_ref[...], preferred_element_type=F32)
    d = k.shape[1] // mem_heads
    for hd in range(mem_heads):
        k_ref[:, hd * d:(hd + 1) * d] = _rms(k[:, hd * d:(hd + 1) * d], gk_ref[...])
    v_ref[...] = jnp.dot(m, wv_ref[...], preferred_element_type=F32)


def _mem_kv(mem, w, mem_heads):
    n, d = mem.shape
    ins = [mem, w["gmem"], w["w_mk"], w["w_mv"], w["gmk"]]
    return pl.pallas_call(
        functools.partial(_mem_kv_kernel, mem_heads),
        grid=(1,),
        in_specs=[_const_spec(a.shape) for a in ins],
        out_specs=[_whole_out_spec((n, d))] * 2,
        out_shape=[jax.ShapeDtypeStruct((n, d), F32)] * 2,
        compiler_params=_params("arbitrary"),
        name="mem_kv",
    )(*ins)


def _mem_attend_kernel(mem_heads, q_ref, k_ref, v_ref, o_ref):
    q = q_ref[0]
    k = k_ref[0].astype(BF16)
    v = v_ref[0].astype(BF16)
    d = q.shape[1] // mem_heads
    for hd in range(mem_heads):
        sl = slice(hd * d, (hd + 1) * d)
        s = _bdot_nt(q[:, sl], k[:, sl])
        p = jnp.exp(s - jnp.max(s, axis=-1, keepdims=True))
        p = p / jnp.sum(p, axis=-1, keepdims=True)
        o_ref[0, :, sl] = jnp.dot(p.astype(BF16), v[:, sl], preferred_element_type=F32).astype(BF16)


def _mem_attend(q, k, v, mem_heads, shared_kv):
    g, tq, d = q.shape
    n_mem = k.shape[1]
    kv_map = (lambda i: (0, 0, 0)) if shared_kv else (lambda i: (i, 0, 0))
    return pl.pallas_call(
        functools.partial(_mem_attend_kernel, mem_heads),
        grid=(g,),
        in_specs=[pl.BlockSpec((1, tq, d), lambda i: (i, 0, 0)),
                  pl.BlockSpec((1, n_mem, d), kv_map), pl.BlockSpec((1, n_mem, d), kv_map)],
        out_specs=pl.BlockSpec((1, tq, d), lambda i: (i, 0, 0)),
        out_shape=jax.ShapeDtypeStruct((g, tq, d), BF16),
        compiler_params=_params("parallel"),
        name="mem_attend_prompt" if shared_kv else "mem_attend_sample",
    )(q, k, v)


def _mem_attend_sample_kernel(q_ref, k_hbm, v_hbm, o_ref, kbuf, vbuf, sems):
    b = pl.program_id(0)
    slot = b % 2
    mem_heads, _, d = kbuf.shape[1:]

    def copies(seq, slot):
        out = []
        for hd in range(mem_heads):
            out.append(pltpu.make_async_copy(k_hbm.at[seq, :, hd, :], kbuf.at[slot, hd], sems.at[0, slot]))
            out.append(pltpu.make_async_copy(v_hbm.at[seq, :, hd, :], vbuf.at[slot, hd], sems.at[1, slot]))
        return out

    @pl.when(b == 0)
    def _():
        for cp in copies(0, 0):
            cp.start()

    @pl.when(b + 1 < pl.num_programs(0))
    def _():
        for cp in copies(b + 1, 1 - slot):
            cp.start()

    for cp in copies(b, slot):
        cp.wait()

    q = q_ref[0]
    scores = [_bdot_nt(q[:, hd * d:(hd + 1) * d], kbuf[slot, hd]) for hd in range(mem_heads)]
    for hd, s in enumerate(scores):
        p = jnp.exp(s - jnp.max(s, axis=-1, keepdims=True))
        p = p / jnp.sum(p, axis=-1, keepdims=True)
        o_ref[0, :, hd * d:(hd + 1) * d] = _bdot(p, vbuf[slot, hd]).astype(BF16)


def _mem_attend_sample(q, k, v):
    g, tq, d = q.shape
    n_mem, mem_heads, mem_dim = k.shape[1:]
    return pl.pallas_call(
        _mem_attend_sample_kernel,
        grid=(g,),
        in_specs=[pl.BlockSpec((1, tq, d), lambda i: (i, 0, 0)),
                  pl.BlockSpec(memory_space=pl.ANY), pl.BlockSpec(memory_space=pl.ANY)],
        out_specs=pl.BlockSpec((1, tq, d), lambda i: (i, 0, 0)),
        out_shape=jax.ShapeDtypeStruct((g, tq, d), BF16),
        scratch_shapes=[pltpu.VMEM((2, mem_heads, n_mem, mem_dim), F32),
                        pltpu.VMEM((2, mem_heads, n_mem, mem_dim), F32),
                        pltpu.SemaphoreType.DMA((2, 2))],
        compiler_params=_params("arbitrary"),
        name="mem_attend_sample",
    )(q, k, v)


def _merge_kernel(x_ref, a_ref, mla_ref, mem_ref, g1_ref, wg_ref, wo_ref, wmo_ref, wout_ref, h_ref):
    x = x_ref[...]
    d = x.shape[1]
    gates = jax.nn.sigmoid(jnp.dot(_rms(x, g1_ref[...]).astype(BF16), wg_ref[...], preferred_element_type=F32))
    br_b = jnp.dot(mla_ref[...], wo_ref[...], preferred_element_type=F32)
    br_c = jnp.dot(mem_ref[...], wmo_ref[...], preferred_element_type=F32)
    mix = gates[:, :d] * a_ref[...] + gates[:, d:2 * d] * br_b + gates[:, 2 * d:] * br_c
    h_ref[...] = x + jnp.dot(mix.astype(BF16), wout_ref[...], preferred_element_type=F32)


def _merge(x, a, mla_o, mem_o, w, w_o, tm):
    n, d = x.shape
    ins = [x, a, mla_o, mem_o, w["g1"], w["w_gate"], w_o, w["w_mo"], w["w_out"]]
    return pl.pallas_call(
        _merge_kernel,
        grid=(n // tm,),
        in_specs=[_row_spec(tm, d), _row_spec(tm, d), _row_spec(tm, mla_o.shape[1]), _row_spec(tm, d)]
        + [_const_spec(a_.shape) for a_ in ins[4:]],
        out_specs=_row_spec(tm, d),
        out_shape=jax.ShapeDtypeStruct((n, d), F32),
        compiler_params=_params("parallel"),
        name="merge",
    )(*ins)


FFN_HALO = 8
FFN_CHUNK = 64


def _ffn_prompt_kernel(conv_w, x_ref, g2_ref, wup_ref, cw_ref, cb_ref, wdown_ref, y_ref, tail_ref, ext_ref, act_ref):
    tm = x_ref.shape[0]
    d_ff = wdown_ref.shape[0]

    @pl.when(pl.program_id(0) == 0)
    def _():
        ext_ref[0:FFN_HALO, :] = jnp.zeros((FFN_HALO, ext_ref.shape[1]), F32)

    x = x_ref[...]
    ext_ref[FFN_HALO:, :] = jnp.dot(_rms(x, g2_ref[...]).astype(BF16), wup_ref[...], preferred_element_type=F32)
    first = FFN_HALO - (conv_w - 1)
    for r0 in range(0, tm, FFN_CHUNK):
        hc = jnp.broadcast_to(cb_ref[...], (FFN_CHUNK, ext_ref.shape[1]))
        for k in range(conv_w):
            hc = hc + cw_ref[k:k + 1, :] * ext_ref[r0 + first + k:r0 + first + k + FFN_CHUNK, :]
        act_ref[r0:r0 + FFN_CHUNK, :] = (jax.nn.silu(hc[:, :d_ff]) * hc[:, d_ff:]).astype(BF16)
    y_ref[...] = x + jnp.dot(act_ref[...], wdown_ref[...], preferred_element_type=F32)
    tail = ext_ref[tm:tm + FFN_HALO, :]
    tail_ref[...] = tail
    ext_ref[0:FFN_HALO, :] = tail


def _ffn_prompt(x, w, tm):
    n, d = x.shape
    d_up = w["w_up"].shape[1]
    conv_w = w["ffn_conv_w"].shape[0]
    ins = [x, w["g2"], w["w_up"], w["ffn_conv_w"], w["ffn_conv_b"], w["w_down"]]
    return pl.pallas_call(
        functools.partial(_ffn_prompt_kernel, conv_w),
        grid=(n // tm,),
        in_specs=[_row_spec(tm, d)] + [_const_spec(a.shape) for a in ins[1:]],
        out_specs=[_row_spec(tm, d), _whole_out_spec((FFN_HALO, d_up))],
        out_shape=[jax.ShapeDtypeStruct((n, d), F32), jax.ShapeDtypeStruct((FFN_HALO, d_up), F32)],
        scratch_shapes=[pltpu.VMEM((tm + FFN_HALO, d_up), F32), pltpu.VMEM((tm, d_up // 2), BF16)],
        compiler_params=_params("arbitrary"),
        name="ffn_prompt",
    )(*ins)


def _ffn_sample_kernel(conv_w, n_new, x_ref, hist_ref, g2_ref, wup_ref, cw_ref, cb_ref, wdown_ref, y_ref, up_ref):
    n_seq = hist_ref.shape[1]
    d_ff = wdown_ref.shape[0]
    x = x_ref[...]
    up_ref[...] = jnp.dot(_rms(x, g2_ref[...]).astype(BF16), wup_ref[...], preferred_element_type=F32)

    def ext(j):
        if j < conv_w - 1:
            return hist_ref[j]
        j -= conv_w - 1
        return up_ref[j * n_seq:(j + 1) * n_seq, :]

    for s in range(n_new):
        hc = jnp.broadcast_to(cb_ref[...], (n_seq, up_ref.shape[1]))
        for k in range(conv_w):
            hc = hc + cw_ref[k:k + 1, :] * ext(s + k)
        act = (jax.nn.silu(hc[:, :d_ff]) * hc[:, d_ff:]).astype(BF16)
        rows = slice(s * n_seq, (s + 1) * n_seq)
        y_ref[rows, :] = x[rows, :] + jnp.dot(act, wdown_ref[...], preferred_element_type=F32)


def _ffn_sample(x, hist_t, w, n_new):
    n, d = x.shape
    d_up = w["w_up"].shape[1]
    conv_w = w["ffn_conv_w"].shape[0]
    ins = [x, hist_t, w["g2"], w["w_up"], w["ffn_conv_w"], w["ffn_conv_b"], w["w_down"]]
    return pl.pallas_call(
        functools.partial(_ffn_sample_kernel, conv_w, n_new),
        grid=(1,),
        in_specs=[_const_spec(a.shape) for a in ins],
        out_specs=[_whole_out_spec((n, d)), _whole_out_spec((n, d_up))],
        out_shape=[jax.ShapeDtypeStruct((n, d), F32), jax.ShapeDtypeStruct((n, d_up), F32)],
        compiler_params=_params("arbitrary"),
        name="ffn_sample",
    )(*ins)


def _pad_heads(w3, width):
    k, h, d = w3.shape
    return jnp.concatenate([w3, jnp.zeros((k, h, width - d), w3.dtype)], axis=-1).reshape(k, h * width)


def _rot_half(w, half):
    return jnp.concatenate([-w[..., half:], w[..., :half]], axis=-1)


def _swap_half(g, half):
    return jnp.concatenate([g[..., half:], g[..., :half]], axis=-1)


def _lane_place(v, offset):
    k, d = v.shape
    return jnp.concatenate([jnp.zeros((k, offset), v.dtype), v, jnp.zeros((k, HEAD_PAD - offset - d), v.dtype)], axis=-1)


def _prep_weights(norm1_g, w_in, q_a_norm_g, w_uq, kv_a_norm_g, w_uk, w_uv, q_norm_g, k_norm_g, w_o_mla, conv_w, conv_b,
                  conv_ln_g, conv_ln_b, w_conv_out, mem_norm_g, w_mk, w_mv, mq_norm_g, mk_norm_g, w_mo, w_out, norm2_g,
                  w_up, ffn_conv_w, ffn_conv_b, w_down, rope, mem_heads):
    hp = HEAD_PAD
    d = w_in.shape[0]
    c_conv = conv_w.shape[1]
    q_lora = w_uq.shape[0]
    kv_lora, n_heads, nope = w_uk.shape
    v_head = w_uv.shape[2]
    half = rope // 2
    o_qa = 2 * c_conv
    o_kva = o_qa + q_lora
    o_mq = o_kva + kv_lora + rope
    o_gate = o_mq + w_mk.shape[1]
    w_pe = w_in[:, o_kva + kv_lora:o_mq]
    w_main = jnp.concatenate([w_in[:, :o_kva + kv_lora], _lane_place(w_pe, nope), _lane_place(_rot_half(w_pe, half), nope),
                              w_in[:, o_mq:o_gate]], axis=1).astype(BF16)
    uq3 = w_uq.reshape(q_lora, n_heads, nope + rope)
    uq_rot3 = jnp.concatenate([jnp.zeros((q_lora, n_heads, nope), F32), _rot_half(uq3[..., nope:], half)], axis=-1)
    row = lambda g: g.reshape(1, -1)
    sel = (jnp.arange(n_heads * hp)[:, None]
           == ((jnp.arange(n_heads * rope) // rope) * hp + nope + jnp.arange(n_heads * rope) % rope)[None, :])
    w_abs = jnp.transpose(w_uk, (1, 2, 0))
    w_abs = jnp.concatenate([w_abs, jnp.zeros((n_heads, hp - nope, kv_lora), F32)], axis=1)
    w_o3 = w_o_mla.reshape(n_heads, v_head, d)
    w_o_pad = jnp.concatenate([w_o3, jnp.zeros((n_heads, hp - v_head, d), F32)], axis=1).reshape(n_heads * hp, d)
    return {
        "g1": row(norm1_g), "w_main": w_main, "w_gate": w_in[:, o_gate:].astype(BF16),
        "gqa": row(q_a_norm_g), "w_uq_pad": _pad_heads(uq3, hp).astype(BF16), "w_uq_rot": _pad_heads(uq_rot3, hp).astype(BF16),
        "gq_pad": _lane_place(row(q_norm_g), 0),
        "gq_rot": _lane_place(_swap_half(row(q_norm_g)[:, nope:], half), nope),
        "gkv": row(kv_a_norm_g), "w_uk_pad": _pad_heads(w_uk, hp).astype(BF16), "w_uv_pad": _pad_heads(w_uv, hp).astype(BF16),
        "gkn_pad": _lane_place(row(k_norm_g)[:, :nope], 0),
        "gkp_pad": _lane_place(row(k_norm_g)[:, nope:], nope),
        "gkp_rot": _lane_place(_swap_half(row(k_norm_g)[:, nope:], half), nope),
        "gmq": row(mq_norm_g), "w_abs": w_abs.astype(BF16), "sel_pe": sel.astype(BF16),
        "w_uv_flat": w_uv.reshape(kv_lora, n_heads * v_head).astype(BF16),
        "w_o_pad": w_o_pad.astype(BF16), "w_o": w_o_mla.astype(BF16),
        "conv_w": conv_w, "conv_b": row(conv_b), "conv_ln_g": row(conv_ln_g), "conv_ln_b": row(conv_ln_b),
        "conv_w8": jnp.broadcast_to(conv_w[:, None, :], (conv_w.shape[0], SUBLANES, c_conv)),
        "conv_b8": jnp.broadcast_to(row(conv_b), (SUBLANES, c_conv)),
        "w_conv_out": w_conv_out.astype(BF16),
        "gmem": row(mem_norm_g), "w_mk": w_mk.astype(BF16), "w_mv": w_mv.astype(BF16), "gmk": row(mk_norm_g),
        "w_mo": w_mo.astype(BF16), "w_out": w_out.astype(BF16),
        "g2": row(norm2_g), "w_up": w_up.astype(BF16), "ffn_conv_w": ffn_conv_w, "ffn_conv_b": row(ffn_conv_b),
        "w_down": w_down.astype(BF16),
    }


def _rope_tables(pos, rope, nope):
    half = rope // 2
    inv_freq = ROPE_THETA ** (-jnp.arange(half, dtype=F32) / half)
    ang = pos.astype(F32)[:, None] * inv_freq[None, :]
    cos, sin = jnp.cos(ang), jnp.sin(ang)
    n = pos.shape[0]
    pad = jnp.zeros((n, HEAD_PAD - nope - rope), F32)
    cos_t = jnp.concatenate([jnp.ones((n, nope), F32), cos, cos, pad], axis=-1)
    sin_t = jnp.concatenate([jnp.zeros((n, nope), F32), sin, sin, pad], axis=-1)
    return cos_t, sin_t


def kernel(x_prompt, x_sample, mem_prompt, cache_ckv, cache_kpe, cache_kscale, page_table, state_conv, state_ffn, cache_mem_k, cache_mem_v, norm1_g, w_in, q_a_norm_g, w_uq, kv_a_norm_g, w_uk, w_uv, q_norm_g, k_norm_g, w_o_mla, conv_w, conv_b, conv_ln_g, conv_ln_b, w_conv_out, mem_norm_g, w_mk, w_mv, mq_norm_g, mk_norm_g, w_mo, w_out, norm2_g, w_up, ffn_conv_w, ffn_conv_b, w_down):
    depth = w_in.shape[0]
    assert depth == 1, "single trunk layer"
    b_p, s_p, d = x_prompt.shape
    assert b_p == 1, "one prompt sequence"
    n_seq, n_new, _ = x_sample.shape
    page_size = cache_ckv.shape[2]
    n_past = page_table.shape[1] * page_size
    kv_lora, n_heads, nope = w_uk.shape[1:]
    rope = cache_kpe.shape[-1]
    v_head = w_uv.shape[-1]
    qk_dim = nope + rope
    mem_heads, mem_dim = cache_mem_k.shape[-2:]
    n_mem = mem_prompt.shape[1]
    c_conv = conv_w.shape[-1]
    conv_width = conv_w.shape[1]
    ffn_width = ffn_conv_w.shape[1]
    d_up = w_up.shape[-1]
    hp = HEAD_PAD
    dims = (c_conv, w_uq.shape[1], kv_lora, n_heads, nope, rope, v_head, mem_heads, mem_dim)

    w = _prep_weights(norm1_g[0], w_in[0], q_a_norm_g[0], w_uq[0], kv_a_norm_g[0], w_uk[0], w_uv[0], q_norm_g[0],
                      k_norm_g[0], w_o_mla[0], conv_w[0], conv_b[0], conv_ln_g[0], conv_ln_b[0], w_conv_out[0],
                      mem_norm_g[0], w_mk[0], w_mv[0], mq_norm_g[0], mk_norm_g[0], w_mo[0], w_out[0], norm2_g[0],
                      w_up[0], ffn_conv_w[0], ffn_conv_b[0], w_down[0], rope, mem_heads)

    tm = min(ROW_TILE, s_p)
    xp = x_prompt.reshape(s_p, d)
    cos_p, sin_p = _rope_tables(jnp.arange(s_p), rope, nope)
    u_p, ckv_p, kpe_p, ksc_p, mq_p, q_p, k_p, v_p = _front_end(xp, cos_p, sin_p, w, dims, False, tm)
    a_p = _conv_prompt(u_p, w, tm)
    mla_p = _mla_prompt(q_p, k_p, v_p, n_heads, v_head, min(ATTN_Q_TILE, s_p), min(ATTN_KV_TILE, s_p))
    mk_p, mv_p = _mem_kv(mem_prompt.reshape(n_mem, d), w, mem_heads)
    mem_o_p = _mem_attend(mq_p.reshape(s_p // tm, tm, d), mk_p[None], mv_p[None], mem_heads, True).reshape(s_p, d)
    h_p = _merge(xp, a_p, mla_p, mem_o_p, w, w["w_o_pad"], tm)
    y_p, ffn_tail = _ffn_prompt(h_p, w, tm)

    n_s = n_seq * n_new
    xs = jnp.transpose(x_sample, (1, 0, 2)).reshape(n_s, d)
    pos_s = jnp.repeat(n_past + jnp.arange(n_new), n_seq)
    cos_s, sin_s = _rope_tables(pos_s, rope, nope)
    tms = min(ROW_TILE, n_s)
    u_s, ckv_s, kpe_s, ksc_s, mq_s, qlat_s, qpe_s, _ = _front_end(xs, cos_s, sin_s, w, dims, True, tms)

    def seq_major(a):
        return jnp.transpose(a.reshape(n_new, n_seq, -1), (1, 0, 2))

    ext_t = jnp.concatenate([jnp.transpose(state_conv[0], (1, 0, 2)), u_s.reshape(n_new, n_seq, c_conv)], axis=0)
    a_s = _conv_sample(ext_t, w, n_new, min(32, n_seq)).reshape(n_s, d)

    ckv_s_b = seq_major(ckv_s)
    kpe_s_b = seq_major(kpe_s)[..., nope:nope + rope]
    ksc_s_b = seq_major(ksc_s)[..., :n_heads]
    pad_rows = (-n_new) % 8
    pad_new = lambda a: jnp.pad(a, ((0, 0), (0, pad_rows), (0, 0)))
    snt = jnp.tile(jnp.transpose(ksc_s_b, (0, 2, 1)), (1, n_new, 1))
    snt = jnp.pad(snt, ((0, 0), (0, 0), (0, pad_rows)))
    ql = seq_major(qlat_s).reshape(n_seq, n_new * n_heads, kv_lora)
    qp = seq_major(qpe_s).reshape(n_seq, n_new * n_heads, rope)
    mla_s = _mla_sample(page_table, ql, qp, pad_new(ckv_s_b), pad_new(kpe_s_b), snt, w["w_uv_flat"],
                        cache_ckv.reshape(cache_ckv.shape[1:]),
                        jnp.swapaxes(cache_kpe.reshape(cache_kpe.shape[1:]), 1, 2),
                        jnp.swapaxes(cache_kscale.reshape(cache_kscale.shape[1:]), 1, 2), n_new, n_heads, v_head)
    mla_s_t = jnp.transpose(mla_s, (1, 0, 2)).reshape(n_s, n_heads * v_head)

    mem_o_s = _mem_attend_sample(seq_major(mq_s), cache_mem_k.reshape(cache_mem_k.shape[1:]),
                                 cache_mem_v.reshape(cache_mem_v.shape[1:]))
    mem_o_s_t = jnp.transpose(mem_o_s, (1, 0, 2)).reshape(n_s, d)
    h_s = _merge(xs, a_s, mla_s_t, mem_o_s_t, w, w["w_o"], tms)
    y_s, up_s = _ffn_sample(h_s, jnp.transpose(state_ffn[0], (1, 0, 2)), w, n_new)

    p_ckv = ckv_p.reshape(1, 1, s_p, kv_lora)
    p_kpe = kpe_p[:, nope:nope + rope].reshape(1, 1, s_p, rope)
    p_ksc = ksc_p[:, :n_heads].reshape(1, 1, s_p, n_heads)
    p_conv = u_p[s_p - (conv_width - 1):].reshape(1, 1, conv_width - 1, c_conv)
    p_ffn = ffn_tail[FFN_HALO - (ffn_width - 1):].reshape(1, 1, ffn_width - 1, d_up)
    p_mem_k = mk_p.reshape(1, 1, n_mem, mem_heads, mem_dim)
    p_mem_v = mv_p.reshape(1, 1, n_mem, mem_heads, mem_dim)
    s_conv = jnp.transpose(ext_t[n_new:], (1, 0, 2))[None]
    ffn_ext = jnp.concatenate([state_ffn[0], seq_major(up_s)], axis=1)
    s_ffn = ffn_ext[:, n_new:][None]
    return (y_p.reshape(1, s_p, d), seq_major(y_s), p_ckv, p_kpe, p_ksc, p_conv, p_ffn, p_mem_k, p_mem_v,
            ckv_s_b[None], kpe_s_b[None], ksc_s_b[None], s_conv, s_ffn)
```

```python
import functools

import jax
import jax.numpy as jnp
from jax import lax
from jax.experimental import pallas as pl
from jax.experimental.pallas import tpu as pltpu

F32 = jnp.float32
BF16 = jnp.bfloat16
EPS = 1e-6
ROPE_THETA = 10000.0
HEAD_PAD = 128
VMEM_LIMIT = 56 * 1024 * 1024
ROW_TILE = 512
ATTN_Q_TILE = 1024
ATTN_KV_TILE = 2048
ATTN_DIAG_STRIPS = 2
PAGES_PER_CHUNK = 32
SAMPLE_SUBCHUNKS = 4
NEG_INF = float("-inf")
LOG2E = 1.4426950408889634


def _const_spec(shape):
    nd = len(shape)
    return pl.BlockSpec(shape, lambda *_: (0,) * nd, pipeline_mode=pl.Buffered(1))


def _whole_out_spec(shape):
    nd = len(shape)
    return pl.BlockSpec(shape, lambda *_: (0,) * nd)


def _row_spec(tm, width):
    return pl.BlockSpec((tm, width), lambda i: (i, 0))


def _params(*sem):
    return pltpu.CompilerParams(dimension_semantics=sem, vmem_limit_bytes=VMEM_LIMIT)


def _bdot(a, b):
    return jnp.dot(a.astype(BF16), b.astype(BF16), preferred_element_type=F32)


def _bdot_nt(a, b):
    return lax.dot_general(a.astype(BF16), b.astype(BF16), (((1,), (1,)), ((), ())), preferred_element_type=F32)


def _start_all(copies, group):
    for n, cp in enumerate(copies):
        cp.start(priority=(n // group) % 2)


def _rms(x, g):
    return x * lax.rsqrt(jnp.mean(x * x, axis=-1, keepdims=True) + EPS) * g


def _front_end_kernel(dims, sample, x_ref, cos_ref, sin_ref, g1_ref, wmain_ref, gqa_ref, wuq_ref, wuqr_ref,
                      gq_ref, gqr_ref, gkv_ref, wuk_ref, gkn_ref, gkp_ref, gkpr_ref, gmq_ref, *rest):
    c_conv, q_lora, kv_lora, n_heads, nope, rope, v_head, mem_heads, mem_dim = dims
    qk_dim = nope + rope
    sm_scale = qk_dim ** -0.5 * LOG2E
    if sample:
        wabs_ref, sel_ref, u_ref, ckv_ref, kpe_ref, ksc_ref, mq_ref, qlat_ref, qpe_ref, qs_ref = rest
    else:
        wuv_ref, u_ref, ckv_ref, kpe_ref, ksc_ref, mq_ref, q_ref, k_ref, v_ref = rest
    hp = HEAD_PAD
    h = _rms(x_ref[...], g1_ref[...]).astype(BF16)
    z = jnp.dot(h, wmain_ref[...], preferred_element_type=F32)
    o_qa = 2 * c_conv
    o_c = o_qa + q_lora
    o_pe = o_c + kv_lora
    o_per = o_pe + hp
    o_mq = o_per + hp
    u_ref[...] = z[:, :c_conv] * jax.nn.sigmoid(z[:, c_conv:o_qa])

    cos_t = cos_ref[...]
    sin_t = sin_ref[...]
    qa = _rms(z[:, o_qa:o_c], gqa_ref[...]).astype(BF16)
    q = jnp.dot(qa, wuq_ref[...], preferred_element_type=F32)
    qr = jnp.dot(qa, wuqr_ref[...], preferred_element_type=F32)
    g_cos = gq_ref[...] * cos_t
    g_sin = gqr_ref[...] * sin_t
    q_out = qs_ref if sample else q_ref
    for hd in range(n_heads):
        sl = slice(hd * hp, (hd + 1) * hp)
        qh = q[:, sl]
        r = lax.rsqrt(jnp.sum(qh * qh, axis=-1, keepdims=True) / qk_dim + EPS)
        qn = r * (qh * g_cos + qr[:, sl] * g_sin) * sm_scale
        q_out[:, sl] = qn.astype(BF16)
        if sample:
            qlat_ref[:, hd * kv_lora:(hd + 1) * kv_lora] = _bdot(qn * gkn_ref[...], wabs_ref[hd]).astype(BF16)
    if sample:
        qpe_ref[...] = jnp.dot(qs_ref[...], sel_ref[...], preferred_element_type=F32).astype(BF16)

    c_kv = _rms(z[:, o_c:o_pe], gkv_ref[...])
    ckv_ref[...] = c_kv
    pe = z[:, o_pe:o_per]
    k_pe = pe * (gkp_ref[...] * cos_t) + z[:, o_per:o_mq] * (gkpr_ref[...] * sin_t)
    kpe_ref[...] = k_pe
    ss_pe = jnp.sum(pe * pe, axis=-1, keepdims=True)
    c_bf = c_kv.astype(BF16)
    k_nope = jnp.dot(c_bf, wuk_ref[...], preferred_element_type=F32)
    lane = lax.broadcasted_iota(jnp.int32, (x_ref.shape[0], hp), 1)
    ksc_all = jnp.zeros((x_ref.shape[0], hp), F32)
    for hd in range(n_heads):
        sl = slice(hd * hp, (hd + 1) * hp)
        kn = k_nope[:, sl]
        ksc = lax.rsqrt((jnp.sum(kn * kn, axis=-1, keepdims=True) + ss_pe) / qk_dim + EPS)
        ksc_all = jnp.where(lane == hd, ksc, ksc_all)
        if not sample:
            k_ref[:, sl] = ((kn * gkn_ref[...] + k_pe) * ksc).astype(BF16)
    ksc_ref[...] = ksc_all
    if not sample:
        v = jnp.dot(c_bf, wuv_ref[...], preferred_element_type=F32)
        head_lane = lax.broadcasted_iota(jnp.int32, v.shape, 1) % hp
        v_ref[...] = jnp.where(head_lane == v_head, 1.0, v).astype(BF16)

    for hd in range(mem_heads):
        sl = slice(hd * mem_dim, (hd + 1) * mem_dim)
        mq_ref[:, sl] = (_rms(z[:, o_mq + hd * mem_dim:o_mq + (hd + 1) * mem_dim], gmq_ref[...])
                         * (mem_dim ** -0.5)).astype(BF16)


def _front_end(x, cos_t, sin_t, w, dims, sample, tm):
    n, d = x.shape
    c_conv, q_lora, kv_lora, n_heads, nope, rope, v_head, mem_heads, mem_dim = dims
    hp = HEAD_PAD
    ins =[x, cos_t, sin_t, w["g1"], w["w_main"], w["gqa"], w["w_uq_pad"], w["w_uq_rot"], w["gq_pad"], w["gq_rot"],
           w["gkv"], w["w_uk_pad"], w["gkn_pad"], w["gkp_pad"], w["gkp_rot"], w["gmq"]]
    ins += [w["w_abs"], w["sel_pe"]] if sample else [w["w_uv_pad"]]
    in_specs = [_row_spec(tm, d), _row_spec(tm, hp), _row_spec(tm, hp)] + [_const_spec(a.shape) for a in ins[3:]]
    outs = [(c_conv, F32), (kv_lora, F32), (hp, F32), (hp, F32), (mem_heads * mem_dim, BF16)]
    if sample:
        outs += [(n_heads * kv_lora, BF16), (n_heads * rope, BF16), (n_heads * hp, BF16)]
    else:
        outs += [(n_heads * hp, BF16)] * 3
    return pl.pallas_call(
        functools.partial(_front_end_kernel, dims, sample),
        grid=(n // tm,),
        in_specs=in_specs,
        out_specs=[_row_spec(tm, wd) for wd, _ in outs],
        out_shape=[jax.ShapeDtypeStruct((n, wd), dt) for wd, dt in outs],
        compiler_params=_params("parallel"),
        name="front_end_sample" if sample else "front_end_prompt",
    )(*ins)


CONV_HALO = 32
CONV_CHUNK = 16
SUBLANES = 8


def _conv_prompt_kernel(conv_w, cur_ref, prev_ref, w_ref, b_ref, lng_ref, lnb_ref, wout_ref, a_ref, ext_ref, sh_ref,
                        act_ref):
    tm = cur_ref.shape[0]
    prev = prev_ref[...]
    ext_ref[0:CONV_HALO, :] = jnp.where(pl.program_id(0) == 0, jnp.zeros_like(prev), prev)
    ext_ref[CONV_HALO:, :] = cur_ref[...]
    span = sh_ref.shape[1]
    for phase in range(1, SUBLANES):
        sh_ref[phase - 1] = ext_ref[phase:phase + span, :]
    first = CONV_HALO - (conv_w - 1)
    groups = CONV_CHUNK // SUBLANES
    for r0 in range(0, tm, CONV_CHUNK):
        accs = [b_ref[...]] * groups
        for k in range(conv_w):
            phase = (first + k) % SUBLANES
            base = r0 + first + k - phase
            w8 = w_ref[k]
            for g in range(groups):
                lo = base + g * SUBLANES
                tap = ext_ref[lo:lo + SUBLANES, :] if phase == 0 else sh_ref[phase - 1, lo:lo + SUBLANES, :]
                accs[g] = accs[g] + w8 * tap
        acc = jnp.concatenate(accs, axis=0)
        mu = jnp.mean(acc, axis=-1, keepdims=True)
        cen = acc - mu
        var = jnp.mean(cen * cen, axis=-1, keepdims=True)
        y = cen * lax.rsqrt(var + EPS) * lng_ref[...] + lnb_ref[...]
        act_ref[r0:r0 + CONV_CHUNK, :] = jax.nn.silu(y).astype(BF16)
    a_ref[...] = jnp.dot(act_ref[...], wout_ref[...], preferred_element_type=F32)


def _conv_prompt(u, w, tm):
    n, c = u.shape
    conv_w = w["conv_w"].shape[0]
    ratio = tm // CONV_HALO
    return pl.pallas_call(
        functools.partial(_conv_prompt_kernel, conv_w),
        grid=(n // tm,),
        in_specs=[_row_spec(tm, c),
                  pl.BlockSpec((CONV_HALO, c), lambda i: (jnp.maximum(i * ratio - 1, 0), 0)),
                  _const_spec(w["conv_w8"].shape), _const_spec((SUBLANES, c)), _const_spec((1, c)), _const_spec((1, c)),
                  _const_spec(w["w_conv_out"].shape)],
        out_specs=_row_spec(tm, c),
        out_shape=jax.ShapeDtypeStruct((n, c), F32),
        scratch_shapes=[pltpu.VMEM((tm + CONV_HALO, c), F32),
                        pltpu.VMEM((SUBLANES - 1, tm + CONV_HALO - SUBLANES, c), F32),
                        pltpu.VMEM((tm, c), BF16)],
        compiler_params=_params("parallel"),
        name="conv_prompt",
    )(u, u, w["conv_w8"], w["conv_b8"], w["conv_ln_g"], w["conv_ln_b"], w["w_conv_out"])


def _conv_sample_kernel(conv_w, n_new, ext_ref, w_ref, b_ref, lng_ref, lnb_ref, wout_ref, a_ref):
    bb, c = ext_ref.shape[1], ext_ref.shape[2]
    for s in range(n_new):
        acc = jnp.broadcast_to(b_ref[...], (bb, c))
        for k in range(conv_w):
            acc = acc + w_ref[k:k + 1, :] * ext_ref[s + k]
        mu = jnp.mean(acc, axis=-1, keepdims=True)
        cen = acc - mu
        var = jnp.mean(cen * cen, axis=-1, keepdims=True)
        y = cen * lax.rsqrt(var + EPS) * lng_ref[...] + lnb_ref[...]
        a_ref[s] = jnp.dot(jax.nn.silu(y).astype(BF16), wout_ref[...], preferred_element_type=F32)


def _conv_sample(ext_t, w, n_new, bb):
    t, b, c = ext_t.shape
    conv_w = w["conv_w"].shape[0]
    return pl.pallas_call(
        functools.partial(_conv_sample_kernel, conv_w, n_new),
        grid=(b // bb,),
        in_specs=[pl.BlockSpec((t, bb, c), lambda i: (0, i, 0)),
                  _const_spec(w["conv_w"].shape), _const_spec((1, c)), _const_spec((1, c)), _const_spec((1, c)),
                  _const_spec(w["w_conv_out"].shape)],
        out_specs=pl.BlockSpec((n_new, bb, c), lambda i: (0, i, 0)),
        out_shape=jax.ShapeDtypeStruct((n_new, b, c), F32),
        compiler_params=_params("parallel"),
        name="conv_sample",
    )(ext_t, w["conv_w"], w["conv_b"], w["conv_ln_g"], w["conv_ln_b"], w["w_conv_out"])


def _mla_prompt_kernel(v_head, tk, q_ref, k_ref, v_ref, o_ref):
    tq = q_ref.shape[0]
    ratio = tk // tq
    i = pl.program_id(1)
    q = q_ref[...]

    def update(start, width, r0, carry, masked):
        m, acc = carry
        s = _bdot_nt(q[r0:], k_ref[pl.ds(start, width), :])
        if masked:
            row = lax.broadcasted_iota(jnp.int32, s.shape, 0)
            col = lax.broadcasted_iota(jnp.int32, s.shape, 1)
            s = jnp.where(col <= row, s, NEG_INF)
        m_new = jnp.maximum(m[r0:], jnp.max(s, axis=-1, keepdims=True))
        p = jnp.exp2(s - m_new).astype(BF16)
        acc_new = jnp.exp2(m[r0:] - m_new) * acc[r0:] + jnp.dot(p, v_ref[pl.ds(start, width), :],
                                                                preferred_element_type=F32)
        if r0:
            m_new = jnp.concatenate([m[:r0], m_new], axis=0)
            acc_new = jnp.concatenate([acc[:r0], acc_new], axis=0)
        return m_new, acc_new

    carry = (jnp.full((tq, 1), NEG_INF, F32), jnp.zeros((tq, v_ref.shape[1]), F32))
    n_wide = i // ratio
    carry = lax.fori_loop(0, n_wide, lambda j, c: update(pl.multiple_of(j * tk, tk), tk, 0, c, False), carry)
    for extra in range(ratio - 1):
        carry = lax.cond(i % ratio > extra,
                         lambda c: update(pl.multiple_of((n_wide * ratio + extra) * tq, tq), tq, 0, c, False),
                         lambda c: c, carry)
    strip = tq // ATTN_DIAG_STRIPS
    for d in range(ATTN_DIAG_STRIPS):
        carry = update(pl.multiple_of(i * tq + d * strip, strip), strip, d * strip, carry, True)
    acc = carry[1]
    o_ref[...] = (acc / acc[:, v_head:v_head + 1]).astype(BF16)


def _mla_prompt(q, k, v, n_heads, v_head, tq, tk):
    n = q.shape[0]
    hp = HEAD_PAD
    assert tk % tq == 0
    return pl.pallas_call(
        functools.partial(_mla_prompt_kernel, v_head, tk),
        grid=(n_heads, n // tq),
        in_specs=[pl.BlockSpec((tq, hp), lambda h, i: (i, h)),
                  pl.BlockSpec((n, hp), lambda h, i: (0, h)),
                  pl.BlockSpec((n, hp), lambda h, i: (0, h))],
        out_specs=pl.BlockSpec((tq, hp), lambda h, i: (i, h)),
        out_shape=jax.ShapeDtypeStruct((n, n_heads * hp), BF16),
        compiler_params=_params("parallel", "arbitrary"),
        name="mla_prompt",
    )(q, k, v)


def _mla_sample_kernel(n_new, n_heads, v_head, pt_ref, ql_ref, qp_ref, cn_ref, kn_ref, snt_ref, wuv_ref,
                       ckv_hbm, kpe_hbm, ksc_hbm, o_ref, cbuf, pbuf, sbuf, sems, m_ref, l_ref, acc_ref):
    n_seq, n_pages = pt_ref.shape
    pages = cbuf.shape[1]
    page_size = cbuf.shape[2]
    chunks_per_seq = n_pages // pages
    n_chunks = n_seq * chunks_per_seq
    rows = ql_ref.shape[1]
    b = pl.program_id(0)
    c = pl.program_id(1)
    t = b * chunks_per_seq + c
    slot = t % 2

    def copies(t, slot):
        b = t // chunks_per_seq
        c = t % chunks_per_seq
        out = []
        for r in range(pages):
            page = pt_ref[b, c * pages + r]
            out.append(pltpu.make_async_copy(ckv_hbm.at[page], cbuf.at[slot, r], sems.at[0, slot]))
            out.append(pltpu.make_async_copy(kpe_hbm.at[page], pbuf.at[slot, r], sems.at[1, slot]))
            out.append(pltpu.make_async_copy(ksc_hbm.at[page], sbuf.at[slot, r], sems.at[2, slot]))
        return out

    def online_update(s, v_bf):
        m = m_ref[...]
        m_new = jnp.maximum(m, jnp.max(s, axis=-1, keepdims=True))
        p = jnp.exp2(s - m_new)
        alpha = jnp.exp2(m - m_new)
        l_ref[...] = alpha * l_ref[...] + jnp.sum(p, axis=-1, keepdims=True)
        acc_ref[...] = alpha * acc_ref[...] + jnp.dot(p.astype(BF16), v_bf, preferred_element_type=F32)
        m_ref[...] = m_new

    @pl.when(t == 0)
    def _():
        _start_all(copies(0, 0), 3)

    @pl.when(t + 1 < n_chunks)
    def _():
        _start_all(copies(t + 1, 1 - slot), 3)

    for cp in copies(t, slot):
        cp.wait()

    @pl.when(c == 0)
    def _():
        m_ref[...] = jnp.full(m_ref.shape, NEG_INF, F32)
        l_ref[...] = jnp.zeros(l_ref.shape, F32)
        acc_ref[...] = jnp.zeros(acc_ref.shape, F32)

    ql = ql_ref[0]
    qp = qp_ref[0]
    sub = pages // SAMPLE_SUBCHUNKS
    parts = []
    for i in range(SAMPLE_SUBCHUNKS):
        pr = range(i * sub, (i + 1) * sub)
        c_bf = cbuf[slot, i * sub:(i + 1) * sub].reshape(sub * page_size, cbuf.shape[3]).astype(BF16)
        kp_t = jnp.concatenate([pbuf[slot, r] for r in pr], axis=1)
        ks_t = jnp.concatenate([sbuf[slot, r] for r in pr], axis=1)
        s = _bdot_nt(ql, c_bf) + _bdot(qp, kp_t)
        parts.append((s * jnp.concatenate([ks_t] * n_new, axis=0), c_bf))
    for s, c_bf in parts:
        online_update(s, c_bf)

    @pl.when(c == chunks_per_seq - 1)
    def _():
        cn = cn_ref[0].astype(BF16).astype(F32)
        kn = kn_ref[0].astype(BF16).astype(F32)
        snt = snt_ref[0]
        qlf = ql.astype(F32)
        qpf = qp.astype(F32)
        q_tok = lax.broadcasted_iota(jnp.int32, (rows, 1), 0) // n_heads
        m = m_ref[...]
        l = l_ref[...]
        acc = acc_ref[...]
        for tk in range(n_new):
            st = (jnp.sum(qlf * cn[tk:tk + 1, :], axis=-1, keepdims=True)
                  + jnp.sum(qpf * kn[tk:tk + 1, :], axis=-1, keepdims=True)) * snt[:, tk:tk + 1]
            vis = q_tok >= tk
            m_new = jnp.where(vis, jnp.maximum(m, st), m)
            p = jnp.where(vis, jnp.exp2(st - m_new), 0.0)
            alpha = jnp.exp2(m - m_new)
            l = alpha * l + p
            acc = alpha * acc + p.astype(BF16).astype(F32) * cn[tk:tk + 1, :]
            m = m_new
        o_lat = (acc / l).astype(BF16)
        full = jnp.dot(o_lat, wuv_ref[...], preferred_element_type=F32)
        r_head = lax.broadcasted_iota(jnp.int32, full.shape, 0) % n_heads
        c_head = lax.broadcasted_iota(jnp.int32, full.shape, 1) // v_head
        full = jnp.where(r_head == c_head, full, 0.0)
        o_ref[0] = jnp.sum(full.reshape(n_new, n_heads, full.shape[1]), axis=1).astype(BF16)


def _mla_sample(page_table, ql, qp, cn, kn, snt, w_uv_flat, ckv, kpe, ksc, n_new, n_heads, v_head):
    n_seq, rows = ql.shape[:2]
    page_size = ckv.shape[1]
    pages = min(PAGES_PER_CHUNK, page_table.shape[1])
    assert page_table.shape[1] % pages == 0 and pages % SAMPLE_SUBCHUNKS == 0
    per_seq = lambda a: pl.BlockSpec((1,) + a.shape[1:], lambda b, c, pt: (b, 0, 0))
    grid_spec = pltpu.PrefetchScalarGridSpec(
        num_scalar_prefetch=1,
        grid=(n_seq, page_table.shape[1] // pages),
        in_specs=[per_seq(ql), per_seq(qp), per_seq(cn), per_seq(kn), per_seq(snt),
                  pl.BlockSpec(w_uv_flat.shape, lambda b, c, pt: (0, 0)),
                  pl.BlockSpec(memory_space=pl.ANY), pl.BlockSpec(memory_space=pl.ANY),
                  pl.BlockSpec(memory_space=pl.ANY)],
        out_specs=pl.BlockSpec((1, n_new, n_heads * v_head), lambda b, c, pt: (b, 0, 0)),
        scratch_shapes=[pltpu.VMEM((2, pages, page_size, ckv.shape[2]), F32),
                        pltpu.VMEM((2, pages, kpe.shape[1], page_size), F32),
                        pltpu.VMEM((2, pages, ksc.shape[1], page_size), F32),
                        pltpu.SemaphoreType.DMA((3, 2)),
                        pltpu.VMEM((rows, 1), F32), pltpu.VMEM((rows, 1), F32),
                        pltpu.VMEM((rows, ckv.shape[2]), F32)])
    return pl.pallas_call(
        functools.partial(_mla_sample_kernel, n_new, n_heads, v_head),
        grid_spec=grid_spec,
        out_shape=jax.ShapeDtypeStruct((n_seq, n_new, n_heads * v_head), BF16),
        compiler_params=_params("arbitrary", "arbitrary"),
        name="mla_sample",
    )(page_table, ql, qp, cn, kn, snt, w_uv_flat, ckv, kpe, ksc)


def _mem_kv_kernel(mem_heads, mem_ref, g_ref, wk_ref, wv_ref, gk_ref, k_ref, v_ref):
    m = _rms(mem_ref[...], g_ref[...]).astype(BF16)
    k = jnp.dot(m, wk_ref[...], preferred_element_type=F32)
    d = k.shape[1] // mem_heads
    for hd in range(mem_heads):
        k_ref[:, hd * d:(hd + 1) * d] = _rms(k[:, hd * d:(hd + 1) * d], gk_ref[...])
    v_ref[...] = jnp.dot(m, wv_ref[...], preferred_element_type=F32)


def _mem_kv(mem, w, mem_heads):
    n, d = mem.shape
    ins = [mem, w["gmem"], w["w_mk"], w["w_mv"], w["gmk"]]
    return pl.pallas_call(
        functools.partial(_mem_kv_kernel, mem_heads),
        grid=(1,),
        in_specs=[_const_spec(a.shape) for a in ins],
        out_specs=[_whole_out_spec((n, d))] * 2,
        out_shape=[jax.ShapeDtypeStruct((n, d), F32)] * 2,
        compiler_params=_params("arbitrary"),
        name="mem_kv",
    )(*ins)


def _mem_attend_kernel(mem_heads, q_ref, k_ref, v_ref, o_ref):
    q = q_ref[0]
    k = k_ref[0].astype(BF16)
    v = v_ref[0].astype(BF16)
    d = q.shape[1] // mem_heads
    for hd in range(mem_heads):
        sl = slice(hd * d, (hd + 1) * d)
        s = _bdot_nt(q[:, sl], k[:, sl])
        p = jnp.exp(s - jnp.max(s, axis=-1, keepdims=True))
        p = p / jnp.sum(p, axis=-1, keepdims=True)
        o_ref[0, :, sl] = jnp.dot(p.astype(BF16), v[:, sl], preferred_element_type=F32).astype(BF16)


def _mem_attend(q, k, v, mem_heads, shared_kv):
    g, tq, d = q.shape
    n_mem = k.shape[1]
    kv_map = (lambda i: (0, 0, 0)) if shared_kv else (lambda i: (i, 0, 0))
    return pl.pallas_call(
        functools.partial(_mem_attend_kernel, mem_heads),
        grid=(g,),
        in_specs=[pl.BlockSpec((1, tq, d), lambda i: (i, 0, 0)),
                  pl.BlockSpec((1, n_mem, d), kv_map), pl.BlockSpec((1, n_mem, d), kv_map)],
        out_specs=pl.BlockSpec((1, tq, d), lambda i: (i, 0, 0)),
        out_shape=jax.ShapeDtypeStruct((g, tq, d), BF16),
        compiler_params=_params("parallel"),
        name="mem_attend_prompt" if shared_kv else "mem_attend_sample",
    )(q, k, v)


def _mem_attend_sample_kernel(q_ref, k_hbm, v_hbm, o_ref, kbuf, vbuf, sems):
    b = pl.program_id(0)
    slot = b % 2
    mem_heads, _, d = kbuf.shape[1:]

    def copies(seq, slot):
        out = []
        for hd in range(mem_heads):
            out.append(pltpu.make_async_copy(k_hbm.at[seq, :, hd, :], kbuf.at[slot, hd], sems.at[0, slot]))
            out.append(pltpu.make_async_copy(v_hbm.at[seq, :, hd, :], vbuf.at[slot, hd], sems.at[1, slot]))
        return out

    @pl.when(b == 0)
    def _():
        _start_all(copies(0, 0), 1)

    @pl.when(b + 1 < pl.num_programs(0))
    def _():
        _start_all(copies(b + 1, 1 - slot), 1)

    for cp in copies(b, slot):
        cp.wait()

    q = q_ref[0]
    scores = [_bdot_nt(q[:, hd * d:(hd + 1) * d], kbuf[slot, hd]) for hd in range(mem_heads)]
    for hd, s in enumerate(scores):
        p = jnp.exp(s - jnp.max(s, axis=-1, keepdims=True))
        p = p / jnp.sum(p, axis=-1, keepdims=True)
        o_ref[0, :, hd * d:(hd + 1) * d] = _bdot(p, vbuf[slot, hd]).astype(BF16)


def _mem_attend_sample(q, k, v):
    g, tq, d = q.shape
    n_mem, mem_heads, mem_dim = k.shape[1:]
    return pl.pallas_call(
        _mem_attend_sample_kernel,
        grid=(g,),
        in_specs=[pl.BlockSpec((1, tq, d), lambda i: (i, 0, 0)),
                  pl.BlockSpec(memory_space=pl.ANY), pl.BlockSpec(memory_space=pl.ANY)],
        out_specs=pl.BlockSpec((1, tq, d), lambda i: (i, 0, 0)),
        out_shape=jax.ShapeDtypeStruct((g, tq, d), BF16),
        scratch_shapes=[pltpu.VMEM((2, mem_heads, n_mem, mem_dim), F32),
                        pltpu.VMEM((2, mem_heads, n_mem, mem_dim), F32),
                        pltpu.SemaphoreType.DMA((2, 2))],
        compiler_params=_params("arbitrary"),
        name="mem_attend_sample",
    )(q, k, v)


def _merge_kernel(x_ref, a_ref, mla_ref, mem_ref, g1_ref, wg_ref, wo_ref, wmo_ref, wout_ref, h_ref):
    x = x_ref[...]
    d = x.shape[1]
    gates = jax.nn.sigmoid(jnp.dot(_rms(x, g1_ref[...]).astype(BF16), wg_ref[...], preferred_element_type=F32))
    br_b = jnp.dot(mla_ref[...], wo_ref[...], preferred_element_type=F32)
    br_c = jnp.dot(mem_ref[...], wmo_ref[...], preferred_element_type=F32)
    mix = gates[:, :d] * a_ref[...] + gates[:, d:2 * d] * br_b + gates[:, 2 * d:] * br_c
    h_ref[...] = x + jnp.dot(mix.astype(BF16), wout_ref[...], preferred_element_type=F32)


def _merge(x, a, mla_o, mem_o, w, w_o, tm):
    n, d = x.shape
    ins = [x, a, mla_o, mem_o, w["g1"], w["w_gate"], w_o, w["w_mo"], w["w_out"]]
    return pl.pallas_call(
        _merge_kernel,
        grid=(n // tm,),
        in_specs=[_row_spec(tm, d), _row_spec(tm, d), _row_spec(tm, mla_o.shape[1]), _row_spec(tm, d)]
        + [_const_spec(a_.shape) for a_ in ins[4:]],
        out_specs=_row_spec(tm, d),
        out_shape=jax.ShapeDtypeStruct((n, d), F32),
        compiler_params=_params("parallel"),
        name="merge",
    )(*ins)


FFN_HALO = 8
FFN_CHUNK = 64


def _ffn_prompt_kernel(conv_w, x_ref, g2_ref, wup_ref, cw_ref, cb_ref, wdown_ref, y_ref, tail_ref, ext_ref, act_ref):
    tm = x_ref.shape[0]
    d_ff = wdown_ref.shape[0]

    @pl.when(pl.program_id(0) == 0)
    def _():
        ext_ref[0:FFN_HALO, :] = jnp.zeros((FFN_HALO, ext_ref.shape[1]), F32)

    x = x_ref[...]
    ext_ref[FFN_HALO:, :] = jnp.dot(_rms(x, g2_ref[...]).astype(BF16), wup_ref[...], preferred_element_type=F32)
    first = FFN_HALO - (conv_w - 1)
    for r0 in range(0, tm, FFN_CHUNK):
        hc = jnp.broadcast_to(cb_ref[...], (FFN_CHUNK, ext_ref.shape[1]))
        for k in range(conv_w):
            hc = hc + cw_ref[k:k + 1, :] * ext_ref[r0 + first + k:r0 + first + k + FFN_CHUNK, :]
        act_ref[r0:r0 + FFN_CHUNK, :] = (jax.nn.silu(hc[:, :d_ff]) * hc[:, d_ff:]).astype(BF16)
    y_ref[...] = x + jnp.dot(act_ref[...], wdown_ref[...], preferred_element_type=F32)
    tail = ext_ref[tm:tm + FFN_HALO, :]
    tail_ref[...] = tail
    ext_ref[0:FFN_HALO, :] = tail


def _ffn_prompt(x, w, tm):
    n, d = x.shape
    d_up = w["w_up"].shape[1]
    conv_w = w["ffn_conv_w"].shape[0]
    ins = [x, w["g2"], w["w_up"], w["ffn_conv_w"], w["ffn_conv_b"], w["w_down"]]
    return pl.pallas_call(
        functools.partial(_ffn_prompt_kernel, conv_w),
        grid=(n // tm,),
        in_specs=[_row_spec(tm, d)] + [_const_spec(a.shape) for a in ins[1:]],
        out_specs=[_row_spec(tm, d), _whole_out_spec((FFN_HALO, d_up))],
        out_shape=[jax.ShapeDtypeStruct((n, d), F32), jax.ShapeDtypeStruct((FFN_HALO, d_up), F32)],
        scratch_shapes=[pltpu.VMEM((tm + FFN_HALO, d_up), F32), pltpu.VMEM((tm, d_up // 2), BF16)],
        compiler_params=_params("arbitrary"),
        name="ffn_prompt",
    )(*ins)


def _ffn_sample_kernel(conv_w, n_new, x_ref, hist_ref, g2_ref, wup_ref, cw_ref, cb_ref, wdown_ref, y_ref, up_ref):
    n_seq = hist_ref.shape[1]
    d_ff = wdown_ref.shape[0]
    x = x_ref[...]
    up_ref[...] = jnp.dot(_rms(x, g2_ref[...]).astype(BF16), wup_ref[...], preferred_element_type=F32)

    def ext(j):
        if j < conv_w - 1:
            return hist_ref[j]
        j -= conv_w - 1
        return up_ref[j * n_seq:(j + 1) * n_seq, :]

    for s in range(n_new):
        hc = jnp.broadcast_to(cb_ref[...], (n_seq, up_ref.shape[1]))
        for k in range(conv_w):
            hc = hc + cw_ref[k:k + 1, :] * ext(s + k)
        act = (jax.nn.silu(hc[:, :d_ff]) * hc[:, d_ff:]).astype(BF16)
        rows = slice(s * n_seq, (s + 1) * n_seq)
        y_ref[rows, :] = x[rows, :] + jnp.dot(act, wdown_ref[...], preferred_element_type=F32)


def _ffn_sample(x, hist_t, w, n_new):
    n, d = x.shape
    d_up = w["w_up"].shape[1]
    conv_w = w["ffn_conv_w"].shape[0]
    ins = [x, hist_t, w["g2"], w["w_up"], w["ffn_conv_w"], w["ffn_conv_b"], w["w_down"]]
    return pl.pallas_call(
        functools.partial(_ffn_sample_kernel, conv_w, n_new),
        grid=(1,),
        in_specs=[_const_spec(a.shape) for a in ins],
        out_specs=[_whole_out_spec((n, d)), _whole_out_spec((n, d_up))],
        out_shape=[jax.ShapeDtypeStruct((n, d), F32), jax.ShapeDtypeStruct((n, d_up), F32)],
        compiler_params=_params("arbitrary"),
        name="ffn_sample",
    )(*ins)


def _pad_heads(w3, width):
    k, h, d = w3.shape
    return jnp.concatenate([w3, jnp.zeros((k, h, width - d), w3.dtype)], axis=-1).reshape(k, h * width)


def _rot_half(w, half):
    return jnp.concatenate([-w[..., half:], w[..., :half]], axis=-1)


def _swap_half(g, half):
    return jnp.concatenate([g[..., half:], g[..., :half]], axis=-1)


def _lane_place(v, offset):
    k, d = v.shape
    return jnp.concatenate([jnp.zeros((k, offset), v.dtype), v, jnp.zeros((k, HEAD_PAD - offset - d), v.dtype)], axis=-1)


def _prep_weights(norm1_g, w_in, q_a_norm_g, w_uq, kv_a_norm_g, w_uk, w_uv, q_norm_g, k_norm_g, w_o_mla, conv_w, conv_b,
                  conv_ln_g, conv_ln_b, w_conv_out, mem_norm_g, w_mk, w_mv, mq_norm_g, mk_norm_g, w_mo, w_out, norm2_g,
                  w_up, ffn_conv_w, ffn_conv_b, w_down, rope, mem_heads):
    hp = HEAD_PAD
    d = w_in.shape[0]
    c_conv = conv_w.shape[1]
    q_lora = w_uq.shape[0]
    kv_lora, n_heads, nope = w_uk.shape
    v_head = w_uv.shape[2]
    half = rope // 2
    o_qa = 2 * c_conv
    o_kva = o_qa + q_lora
    o_mq = o_kva + kv_lora + rope
    o_gate = o_mq + w_mk.shape[1]
    w_pe = w_in[:, o_kva + kv_lora:o_mq]
    w_main = jnp.concatenate([w_in[:, :o_kva + kv_lora], _lane_place(w_pe, nope), _lane_place(_rot_half(w_pe, half), nope),
                              w_in[:, o_mq:o_gate]], axis=1).astype(BF16)
    uq3 = w_uq.reshape(q_lora, n_heads, nope + rope)
    uq_rot3 = jnp.concatenate([jnp.zeros((q_lora, n_heads, nope), F32), _rot_half(uq3[..., nope:], half)], axis=-1)
    row = lambda g: g.reshape(1, -1)
    sel = (jnp.arange(n_heads * hp)[:, None]
           == ((jnp.arange(n_heads * rope) // rope) * hp + nope + jnp.arange(n_heads * rope) % rope)[None, :])
    w_abs = jnp.transpose(w_uk, (1, 2, 0))
    w_abs = jnp.concatenate([w_abs, jnp.zeros((n_heads, hp - nope, kv_lora), F32)], axis=1)
    w_o3 = w_o_mla.reshape(n_heads, v_head, d)
    w_o_pad = jnp.concatenate([w_o3, jnp.zeros((n_heads, hp - v_head, d), F32)], axis=1).reshape(n_heads * hp, d)
    return {
        "g1": row(norm1_g), "w_main": w_main, "w_gate": w_in[:, o_gate:].astype(BF16),
        "gqa": row(q_a_norm_g), "w_uq_pad": _pad_heads(uq3, hp).astype(BF16), "w_uq_rot": _pad_heads(uq_rot3, hp).astype(BF16),
        "gq_pad": _lane_place(row(q_norm_g), 0),
        "gq_rot": _lane_place(_swap_half(row(q_norm_g)[:, nope:], half), nope),
        "gkv": row(kv_a_norm_g), "w_uk_pad": _pad_heads(w_uk, hp).astype(BF16), "w_uv_pad": _pad_heads(w_uv, hp).astype(BF16),
        "gkn_pad": _lane_place(row(k_norm_g)[:, :nope], 0),
        "gkp_pad": _lane_place(row(k_norm_g)[:, nope:], nope),
        "gkp_rot": _lane_place(_swap_half(row(k_norm_g)[:, nope:], half), nope),
        "gmq": row(mq_norm_g), "w_abs": w_abs.astype(BF16), "sel_pe": sel.astype(BF16),
        "w_uv_flat": w_uv.reshape(kv_lora, n_heads * v_head).astype(BF16),
        "w_o_pad": w_o_pad.astype(BF16), "w_o": w_o_mla.astype(BF16),
        "conv_w": conv_w, "conv_b": row(conv_b), "conv_ln_g": row(conv_ln_g), "conv_ln_b": row(conv_ln_b),
        "conv_w8": jnp.broadcast_to(conv_w[:, None, :], (conv_w.shape[0], SUBLANES, c_conv)),
        "conv_b8": jnp.broadcast_to(row(conv_b), (SUBLANES, c_conv)),
        "w_conv_out": w_conv_out.astype(BF16),
        "gmem": row(mem_norm_g), "w_mk": w_mk.astype(BF16), "w_mv": w_mv.astype(BF16), "gmk": row(mk_norm_g),
        "w_mo": w_mo.astype(BF16), "w_out": w_out.astype(BF16),
        "g2": row(norm2_g), "w_up": w_up.astype(BF16), "ffn_conv_w": ffn_conv_w, "ffn_conv_b": row(ffn_conv_b),
        "w_down": w_down.astype(BF16),
    }


def _rope_tables(pos, rope, nope):
    half = rope // 2
    inv_freq = ROPE_THETA ** (-jnp.arange(half, dtype=F32) / half)
    ang = pos.astype(F32)[:, None] * inv_freq[None, :]
    cos, sin = jnp.cos(ang), jnp.sin(ang)
    n = pos.shape[0]
    pad = jnp.zeros((n, HEAD_PAD - nope - rope), F32)
    cos_t = jnp.concatenate([jnp.ones((n, nope), F32), cos, cos, pad], axis=-1)
    sin_t = jnp.concatenate([jnp.zeros((n, nope), F32), sin, sin, pad], axis=-1)
    return cos_t, sin_t


def kernel(x_prompt, x_sample, mem_prompt, cache_ckv, cache_kpe, cache_kscale, page_table, state_conv, state_ffn, cache_mem_k, cache_mem_v, norm1_g, w_in, q_a_norm_g, w_uq, kv_a_norm_g, w_uk, w_uv, q_norm_g, k_norm_g, w_o_mla, conv_w, conv_b, conv_ln_g, conv_ln_b, w_conv_out, mem_norm_g, w_mk, w_mv, mq_norm_g, mk_norm_g, w_mo, w_out, norm2_g, w_up, ffn_conv_w, ffn_conv_b, w_down):
    depth = w_in.shape[0]
    assert depth == 1, "single trunk layer"
    b_p, s_p, d = x_prompt.shape
    assert b_p == 1, "one prompt sequence"
    n_seq, n_new, _ = x_sample.shape
    page_size = cache_ckv.shape[2]
    n_past = page_table.shape[1] * page_size
    kv_lora, n_heads, nope = w_uk.shape[1:]
    rope = cache_kpe.shape[-1]
    v_head = w_uv.shape[-1]
    qk_dim = nope + rope
    mem_heads, mem_dim = cache_mem_k.shape[-2:]
    n_mem = mem_prompt.shape[1]
    c_conv = conv_w.shape[-1]
    conv_width = conv_w.shape[1]
    ffn_width = ffn_conv_w.shape[1]
    d_up = w_up.shape[-1]
    hp = HEAD_PAD
    dims = (c_conv, w_uq.shape[1], kv_lora, n_heads, nope, rope, v_head, mem_heads, mem_dim)

    w = _prep_weights(norm1_g[0], w_in[0], q_a_norm_g[0], w_uq[0], kv_a_norm_g[0], w_uk[0], w_uv[0], q_norm_g[0],
                      k_norm_g[0], w_o_mla[0], conv_w[0], conv_b[0], conv_ln_g[0], conv_ln_b[0], w_conv_out[0],
                      mem_norm_g[0], w_mk[0], w_mv[0], mq_norm_g[0], mk_norm_g[0], w_mo[0], w_out[0], norm2_g[0],
                      w_up[0], ffn_conv_w[0], ffn_conv_b[0], w_down[0], rope, mem_heads)

    tm = min(ROW_TILE, s_p)
    xp = x_prompt.reshape(s_p, d)
    cos_p, sin_p = _rope_tables(jnp.arange(s_p), rope, nope)
    u_p, ckv_p, kpe_p, ksc_p, mq_p, q_p, k_p, v_p = _front_end(xp, cos_p, sin_p, w, dims, False, tm)
    a_p = _conv_prompt(u_p, w, tm)
    tq = min(ATTN_Q_TILE, s_p)
    mla_p = _mla_prompt(q_p, k_p, v_p, n_heads, v_head, tq, max(tq, min(ATTN_KV_TILE, s_p)))
    mk_p, mv_p = _mem_kv(mem_prompt.reshape(n_mem, d), w, mem_heads)
    mem_o_p = _mem_attend(mq_p.reshape(s_p // tm, tm, d), mk_p[None], mv_p[None], mem_heads, True).reshape(s_p, d)
    h_p = _merge(xp, a_p, mla_p, mem_o_p, w, w["w_o_pad"], tm)
    y_p, ffn_tail = _ffn_prompt(h_p, w, tm)

    n_s = n_seq * n_new
    xs = jnp.transpose(x_sample, (1, 0, 2)).reshape(n_s, d)
    pos_s = jnp.repeat(n_past + jnp.arange(n_new), n_seq)
    cos_s, sin_s = _rope_tables(pos_s, rope, nope)
    tms = min(ROW_TILE, n_s)
    u_s, ckv_s, kpe_s, ksc_s, mq_s, qlat_s, qpe_s, _ = _front_end(xs, cos_s, sin_s, w, dims, True, tms)

    def seq_major(a):
        return jnp.transpose(a.reshape(n_new, n_seq, -1), (1, 0, 2))

    ext_t = jnp.concatenate([jnp.transpose(state_conv[0], (1, 0, 2)), u_s.reshape(n_new, n_seq, c_conv)], axis=0)
    a_s = _conv_sample(ext_t, w, n_new, min(32, n_seq)).reshape(n_s, d)

    ckv_s_b = seq_major(ckv_s)
    kpe_s_b = seq_major(kpe_s)[..., nope:nope + rope]
    ksc_s_b = seq_major(ksc_s)[..., :n_heads]
    pad_rows = (-n_new) % 8
    pad_new = lambda a: jnp.pad(a, ((0, 0), (0, pad_rows), (0, 0)))
    snt = jnp.tile(jnp.transpose(ksc_s_b, (0, 2, 1)), (1, n_new, 1))
    snt = jnp.pad(snt, ((0, 0), (0, 0), (0, pad_rows)))
    ql = seq_major(qlat_s).reshape(n_seq, n_new * n_heads, kv_lora)
    qp = seq_major(qpe_s).reshape(n_seq, n_new * n_heads, rope)
    mla_s = _mla_sample(page_table, ql, qp, pad_new(ckv_s_b), pad_new(kpe_s_b), snt, w["w_uv_flat"],
                        cache_ckv.reshape(cache_ckv.shape[1:]),
                        jnp.swapaxes(cache_kpe.reshape(cache_kpe.shape[1:]), 1, 2),
                        jnp.swapaxes(cache_kscale.reshape(cache_kscale.shape[1:]), 1, 2), n_new, n_heads, v_head)
    mla_s_t = jnp.transpose(mla_s, (1, 0, 2)).reshape(n_s, n_heads * v_head)

    mem_o_s = _mem_attend_sample(seq_major(mq_s), cache_mem_k.reshape(cache_mem_k.shape[1:]),
                                 cache_mem_v.reshape(cache_mem_v.shape[1:]))
    mem_o_s_t = jnp.transpose(mem_o_s, (1, 0, 2)).reshape(n_s, d)
    h_s = _merge(xs, a_s, mla_s_t, mem_o_s_t, w, w["w_o"], tms)
    y_s, up_s = _ffn_sample(h_s, jnp.transpose(state_ffn[0], (1, 0, 2)), w, n_new)

    p_ckv = ckv_p.reshape(1, 1, s_p, kv_lora)
    p_kpe = kpe_p[:, nope:nope + rope].reshape(1, 1, s_p, rope)
    p_ksc = ksc_p[:, :n_heads].reshape(1, 1, s_p, n_heads)
    p_conv = u_p[s_p - (conv_width - 1):].reshape(1, 1, conv_width - 1, c_conv)
    p_ffn = ffn_tail[FFN_HALO - (ffn_width - 1):].reshape(1, 1, ffn_width - 1, d_up)
    p_mem_k = mk_p.reshape(1, 1, n_mem, mem_heads, mem_dim)
    p_mem_v = mv_p.reshape(1, 1, n_mem, mem_heads, mem_dim)
    s_conv = jnp.transpose(ext_t[n_new:], (1, 0, 2))[None]
    ffn_ext = jnp.concatenate([state_ffn[0], seq_major(up_s)], axis=1)
    s_ffn = ffn_ext[:, n_new:][None]
    return (y_p.reshape(1, s_p, d), seq_major(y_s), p_ckv, p_kpe, p_ksc, p_conv, p_ffn, p_mem_k, p_mem_v,
            ckv_s_b[None], kpe_s_b[None], ksc_s_b[None], s_conv, s_ffn)
```

```python
import functools

import jax
import jax.numpy as jnp
from jax import lax
from jax.experimental import pallas as pl
from jax.experimental.pallas import tpu as pltpu

F32 = jnp.float32
BF16 = jnp.bfloat16
EPS = 1e-6
ROPE_THETA = 10000.0
HEAD_PAD = 128
VMEM_LIMIT = 56 * 1024 * 1024
ROW_TILE = 512
ATTN_Q_TILE = 2048
ATTN_KV_TILE = 2048
ATTN_DIAG_STRIPS = 2
PAGES_PER_CHUNK = 64
SAMPLE_SUBCHUNKS = 8
NEG_INF = float("-inf")
LOG2E = 1.4426950408889634


def _const_spec(shape):
    nd = len(shape)
    return pl.BlockSpec(shape, lambda *_: (0,) * nd, pipeline_mode=pl.Buffered(1))


def _whole_out_spec(shape):
    nd = len(shape)
    return pl.BlockSpec(shape, lambda *_: (0,) * nd)


def _row_spec(tm, width):
    return pl.BlockSpec((tm, width), lambda i: (i, 0))


def _params(*sem):
    return pltpu.CompilerParams(dimension_semantics=sem, vmem_limit_bytes=VMEM_LIMIT)


def _bdot(a, b):
    return jnp.dot(a.astype(BF16), b.astype(BF16), preferred_element_type=F32)


def _bdot_nt(a, b):
    return lax.dot_general(a.astype(BF16), b.astype(BF16), (((1,), (1,)), ((), ())), preferred_element_type=F32)


def _start_all(copies, group):
    for n, cp in enumerate(copies):
        cp.start(priority=(n // group) % 2)


def _rms(x, g):
    return x * lax.rsqrt(jnp.mean(x * x, axis=-1, keepdims=True) + EPS) * g


def _front_end_kernel(dims, sample, x_ref, cos_ref, sin_ref, g1_ref, wmain_ref, gqa_ref, wuq_ref, wuqr_ref,
                      gq_ref, gqr_ref, gkv_ref, wuk_ref, gkn_ref, gkp_ref, gkpr_ref, gmq_ref, *rest):
    c_conv, q_lora, kv_lora, n_heads, nope, rope, v_head, mem_heads, mem_dim = dims
    qk_dim = nope + rope
    sm_scale = qk_dim ** -0.5 * LOG2E
    if sample:
        wabs_ref, sel_ref, u_ref, ckv_ref, kpe_ref, ksc_ref, mq_ref, qlat_ref, qpe_ref, qs_ref = rest
    else:
        wuv_ref, u_ref, ckv_ref, kpe_ref, ksc_ref, mq_ref, q_ref, k_ref, v_ref = rest
    hp = HEAD_PAD
    h = _rms(x_ref[...], g1_ref[...]).astype(BF16)
    z = jnp.dot(h, wmain_ref[...], preferred_element_type=F32)
    o_qa = 2 * c_conv
    o_c = o_qa + q_lora
    o_pe = o_c + kv_lora
    o_per = o_pe + hp
    o_mq = o_per + hp
    u_ref[...] = z[:, :c_conv] * jax.nn.sigmoid(z[:, c_conv:o_qa])

    cos_t = cos_ref[...]
    sin_t = sin_ref[...]
    qa = _rms(z[:, o_qa:o_c], gqa_ref[...]).astype(BF16)
    q = jnp.dot(qa, wuq_ref[...], preferred_element_type=F32)
    qr = jnp.dot(qa, wuqr_ref[...], preferred_element_type=F32)
    g_cos = gq_ref[...] * cos_t
    g_sin = gqr_ref[...] * sin_t
    q_out = qs_ref if sample else q_ref
    for hd in range(n_heads):
        sl = slice(hd * hp, (hd + 1) * hp)
        qh = q[:, sl]
        r = lax.rsqrt(jnp.sum(qh * qh, axis=-1, keepdims=True) / qk_dim + EPS)
        qn = r * (qh * g_cos + qr[:, sl] * g_sin) * sm_scale
        q_out[:, sl] = qn.astype(BF16)
        if sample:
            qlat_ref[:, hd * kv_lora:(hd + 1) * kv_lora] = _bdot(qn * gkn_ref[...], wabs_ref[hd]).astype(BF16)
    if sample:
        qpe_ref[...] = jnp.dot(qs_ref[...], sel_ref[...], preferred_element_type=F32).astype(BF16)

    c_kv = _rms(z[:, o_c:o_pe], gkv_ref[...])
    ckv_ref[...] = c_kv
    pe = z[:, o_pe:o_per]
    k_pe = pe * (gkp_ref[...] * cos_t) + z[:, o_per:o_mq] * (gkpr_ref[...] * sin_t)
    kpe_ref[...] = k_pe
    ss_pe = jnp.sum(pe * pe, axis=-1, keepdims=True)
    c_bf = c_kv.astype(BF16)
    k_nope = jnp.dot(c_bf, wuk_ref[...], preferred_element_type=F32)
    lane = lax.broadcasted_iota(jnp.int32, (x_ref.shape[0], hp), 1)
    ksc_all = jnp.zeros((x_ref.shape[0], hp), F32)
    for hd in range(n_heads):
        sl = slice(hd * hp, (hd + 1) * hp)
        kn = k_nope[:, sl]
        ksc = lax.rsqrt((jnp.sum(kn * kn, axis=-1, keepdims=True) + ss_pe) / qk_dim + EPS)
        ksc_all = jnp.where(lane == hd, ksc, ksc_all)
        if not sample:
            k_ref[:, sl] = ((kn * gkn_ref[...] + k_pe) * ksc).astype(BF16)
    ksc_ref[...] = ksc_all
    if not sample:
        v = jnp.dot(c_bf, wuv_ref[...], preferred_element_type=F32)
        head_lane = lax.broadcasted_iota(jnp.int32, v.shape, 1) % hp
        v_ref[...] = jnp.where(head_lane == v_head, 1.0, v).astype(BF16)

    for hd in range(mem_heads):
        sl = slice(hd * mem_dim, (hd + 1) * mem_dim)
        mq_ref[:, sl] = (_rms(z[:, o_mq + hd * mem_dim:o_mq + (hd + 1) * mem_dim], gmq_ref[...])
                         * (mem_dim ** -0.5)).astype(BF16)


def _front_end(x, cos_t, sin_t, w, dims, sample, tm):
    n, d = x.shape
    c_conv, q_lora, kv_lora, n_heads, nope, rope, v_head, mem_heads, mem_dim = dims
    hp = HEAD_PAD
    ins =[x, cos_t, sin_t, w["g1"], w["w_main"], w["gqa"], w["w_uq_pad"], w["w_uq_rot"], w["gq_pad"], w["gq_rot"],
           w["gkv"], w["w_uk_pad"], w["gkn_pad"], w["gkp_pad"], w["gkp_rot"], w["gmq"]]
    ins += [w["w_abs"], w["sel_pe"]] if sample else [w["w_uv_pad"]]
    in_specs = [_row_spec(tm, d), _row_spec(tm, hp), _row_spec(tm, hp)] + [_const_spec(a.shape) for a in ins[3:]]
    outs = [(c_conv, F32), (kv_lora, F32), (hp, F32), (hp, F32), (mem_heads * mem_dim, BF16)]
    if sample:
        outs += [(n_heads * kv_lora, BF16), (n_heads * rope, BF16), (n_heads * hp, BF16)]
    else:
        outs += [(n_heads * hp, BF16)] * 3
    return pl.pallas_call(
        functools.partial(_front_end_kernel, dims, sample),
        grid=(n // tm,),
        in_specs=in_specs,
        out_specs=[_row_spec(tm, wd) for wd, _ in outs],
        out_shape=[jax.ShapeDtypeStruct((n, wd), dt) for wd, dt in outs],
        compiler_params=_params("parallel"),
        name="front_end_sample" if sample else "front_end_prompt",
    )(*ins)


CONV_HALO = 32
CONV_CHUNK = 16
SUBLANES = 8


def _conv_prompt_kernel(conv_w, cur_ref, prev_ref, w_ref, b_ref, lng_ref, lnb_ref, wout_ref, a_ref, ext_ref, sh_ref,
                        act_ref):
    tm = cur_ref.shape[0]
    prev = prev_ref[...]
    ext_ref[0:CONV_HALO, :] = jnp.where(pl.program_id(0) == 0, jnp.zeros_like(prev), prev)
    ext_ref[CONV_HALO:, :] = cur_ref[...]
    span = sh_ref.shape[1]
    for phase in range(1, SUBLANES):
        sh_ref[phase - 1] = ext_ref[phase:phase + span, :]
    first = CONV_HALO - (conv_w - 1)
    groups = CONV_CHUNK // SUBLANES
    for r0 in range(0, tm, CONV_CHUNK):
        accs = [b_ref[...]] * groups
        for k in range(conv_w):
            phase = (first + k) % SUBLANES
            base = r0 + first + k - phase
            w8 = w_ref[k]
            for g in range(groups):
                lo = base + g * SUBLANES
                tap = ext_ref[lo:lo + SUBLANES, :] if phase == 0 else sh_ref[phase - 1, lo:lo + SUBLANES, :]
                accs[g] = accs[g] + w8 * tap
        acc = jnp.concatenate(accs, axis=0)
        mu = jnp.mean(acc, axis=-1, keepdims=True)
        cen = acc - mu
        var = jnp.mean(cen * cen, axis=-1, keepdims=True)
        y = cen * lax.rsqrt(var + EPS) * lng_ref[...] + lnb_ref[...]
        act_ref[r0:r0 + CONV_CHUNK, :] = jax.nn.silu(y).astype(BF16)
    a_ref[...] = jnp.dot(act_ref[...], wout_ref[...], preferred_element_type=F32)


def _conv_prompt(u, w, tm):
    n, c = u.shape
    conv_w = w["conv_w"].shape[0]
    ratio = tm // CONV_HALO
    return pl.pallas_call(
        functools.partial(_conv_prompt_kernel, conv_w),
        grid=(n // tm,),
        in_specs=[_row_spec(tm, c),
                  pl.BlockSpec((CONV_HALO, c), lambda i: (jnp.maximum(i * ratio - 1, 0), 0)),
                  _const_spec(w["conv_w8"].shape), _const_spec((SUBLANES, c)), _const_spec((1, c)), _const_spec((1, c)),
                  _const_spec(w["w_conv_out"].shape)],
        out_specs=_row_spec(tm, c),
        out_shape=jax.ShapeDtypeStruct((n, c), F32),
        scratch_shapes=[pltpu.VMEM((tm + CONV_HALO, c), F32),
                        pltpu.VMEM((SUBLANES - 1, tm + CONV_HALO - SUBLANES, c), F32),
                        pltpu.VMEM((tm, c), BF16)],
        compiler_params=_params("parallel"),
        name="conv_prompt",
    )(u, u, w["conv_w8"], w["conv_b8"], w["conv_ln_g"], w["conv_ln_b"], w["w_conv_out"])


def _conv_sample_kernel(conv_w, n_new, ext_ref, w_ref, b_ref, lng_ref, lnb_ref, wout_ref, a_ref):
    bb, c = ext_ref.shape[1], ext_ref.shape[2]
    for s in range(n_new):
        acc = jnp.broadcast_to(b_ref[...], (bb, c))
        for k in range(conv_w):
            acc = acc + w_ref[k:k + 1, :] * ext_ref[s + k]
        mu = jnp.mean(acc, axis=-1, keepdims=True)
        cen = acc - mu
        var = jnp.mean(cen * cen, axis=-1, keepdims=True)
        y = cen * lax.rsqrt(var + EPS) * lng_ref[...] + lnb_ref[...]
        a_ref[s] = jnp.dot(jax.nn.silu(y).astype(BF16), wout_ref[...], preferred_element_type=F32)


def _conv_sample(ext_t, w, n_new, bb):
    t, b, c = ext_t.shape
    conv_w = w["conv_w"].shape[0]
    return pl.pallas_call(
        functools.partial(_conv_sample_kernel, conv_w, n_new),
        grid=(b // bb,),
        in_specs=[pl.BlockSpec((t, bb, c), lambda i: (0, i, 0)),
                  _const_spec(w["conv_w"].shape), _const_spec((1, c)), _const_spec((1, c)), _const_spec((1, c)),
                  _const_spec(w["w_conv_out"].shape)],
        out_specs=pl.BlockSpec((n_new, bb, c), lambda i: (0, i, 0)),
        out_shape=jax.ShapeDtypeStruct((n_new, b, c), F32),
        compiler_params=_params("parallel"),
        name="conv_sample",
    )(ext_t, w["conv_w"], w["conv_b"], w["conv_ln_g"], w["conv_ln_b"], w["w_conv_out"])


def _mla_prompt_kernel(v_head, tk, q_ref, k_ref, v_ref, o_ref):
    tq = q_ref.shape[0]
    ratio = tk // tq
    i = pl.program_id(1)
    q = q_ref[...]

    def update(start, width, r0, carry, masked):
        m, acc = carry
        s = _bdot_nt(q[r0:], k_ref[pl.ds(start, width), :])
        if masked:
            row = lax.broadcasted_iota(jnp.int32, s.shape, 0)
            col = lax.broadcasted_iota(jnp.int32, s.shape, 1)
            s = jnp.where(col <= row, s, NEG_INF)
        m_new = jnp.maximum(m[r0:], jnp.max(s, axis=-1, keepdims=True))
        p = jnp.exp2(s - m_new).astype(BF16)
        acc_new = jnp.exp2(m[r0:] - m_new) * acc[r0:] + jnp.dot(p, v_ref[pl.ds(start, width), :],
                                                                preferred_element_type=F32)
        if r0:
            m_new = jnp.concatenate([m[:r0], m_new], axis=0)
            acc_new = jnp.concatenate([acc[:r0], acc_new], axis=0)
        return m_new, acc_new

    carry = (jnp.full((tq, 1), NEG_INF, F32), jnp.zeros((tq, v_ref.shape[1]), F32))
    n_wide = i // ratio
    carry = lax.fori_loop(0, n_wide, lambda j, c: update(pl.multiple_of(j * tk, tk), tk, 0, c, False), carry)
    for extra in range(ratio - 1):
        carry = lax.cond(i % ratio > extra,
                         lambda c: update(pl.multiple_of((n_wide * ratio + extra) * tq, tq), tq, 0, c, False),
                         lambda c: c, carry)
    strip = tq // ATTN_DIAG_STRIPS
    for d in range(ATTN_DIAG_STRIPS):
        carry = update(pl.multiple_of(i * tq + d * strip, strip), strip, d * strip, carry, True)
    acc = carry[1]
    o_ref[...] = (acc / acc[:, v_head:v_head + 1]).astype(BF16)


def _mla_prompt(q, k, v, n_heads, v_head, tq, tk):
    n = q.shape[0]
    hp = HEAD_PAD
    assert tk % tq == 0
    return pl.pallas_call(
        functools.partial(_mla_prompt_kernel, v_head, tk),
        grid=(n_heads, n // tq),
        in_specs=[pl.BlockSpec((tq, hp), lambda h, i: (i, h)),
                  pl.BlockSpec((n, hp), lambda h, i: (0, h)),
                  pl.BlockSpec((n, hp), lambda h, i: (0, h))],
        out_specs=pl.BlockSpec((tq, hp), lambda h, i: (i, h)),
        out_shape=jax.ShapeDtypeStruct((n, n_heads * hp), BF16),
        compiler_params=_params("parallel", "arbitrary"),
        name="mla_prompt",
    )(q, k, v)


def _mla_sample_kernel(n_new, n_heads, v_head, pt_ref, ql_ref, qp_ref, cn_ref, kn_ref, snt_ref, wuv_ref,
                       ckv_hbm, kpe_hbm, ksc_hbm, o_ref, cbuf, pbuf, sbuf, sems, m_ref, l_ref, acc_ref):
    n_seq, n_pages = pt_ref.shape
    pages = cbuf.shape[1]
    page_size = cbuf.shape[2]
    chunks_per_seq = n_pages // pages
    n_chunks = n_seq * chunks_per_seq
    rows = ql_ref.shape[1]
    b = pl.program_id(0)
    c = pl.program_id(1)
    t = b * chunks_per_seq + c
    slot = t % 2

    def copies(t, slot):
        b = t // chunks_per_seq
        c = t % chunks_per_seq
        out = []
        for r in range(pages):
            page = pt_ref[b, c * pages + r]
            out.append(pltpu.make_async_copy(ckv_hbm.at[page], cbuf.at[slot, r], sems.at[0, slot]))
            out.append(pltpu.make_async_copy(kpe_hbm.at[page], pbuf.at[slot, r], sems.at[1, slot]))
            out.append(pltpu.make_async_copy(ksc_hbm.at[page], sbuf.at[slot, r], sems.at[2, slot]))
        return out

    def online_update(s, v_bf):
        m = m_ref[...]
        m_new = jnp.maximum(m, jnp.max(s, axis=-1, keepdims=True))
        p = jnp.exp2(s - m_new)
        alpha = jnp.exp2(m - m_new)
        l_ref[...] = alpha * l_ref[...] + jnp.sum(p, axis=-1, keepdims=True)
        acc_ref[...] = alpha * acc_ref[...] + jnp.dot(p.astype(BF16), v_bf, preferred_element_type=F32)
        m_ref[...] = m_new

    @pl.when(t == 0)
    def _():
        _start_all(copies(0, 0), 3)

    @pl.when(t + 1 < n_chunks)
    def _():
        _start_all(copies(t + 1, 1 - slot), 3)

    for cp in copies(t, slot):
        cp.wait()

    @pl.when(c == 0)
    def _():
        m_ref[...] = jnp.full(m_ref.shape, NEG_INF, F32)
        l_ref[...] = jnp.zeros(l_ref.shape, F32)
        acc_ref[...] = jnp.zeros(acc_ref.shape, F32)

    ql = ql_ref[0]
    qp = qp_ref[0]
    sub = pages // SAMPLE_SUBCHUNKS
    parts = []
    for i in range(SAMPLE_SUBCHUNKS):
        pr = range(i * sub, (i + 1) * sub)
        c_bf = cbuf[slot, i * sub:(i + 1) * sub].reshape(sub * page_size, cbuf.shape[3]).astype(BF16)
        kp_t = jnp.concatenate([pbuf[slot, r] for r in pr], axis=1)
        ks_t = jnp.concatenate([sbuf[slot, r] for r in pr], axis=1)
        s = _bdot_nt(ql, c_bf) + _bdot(qp, kp_t)
        parts.append((s * jnp.concatenate([ks_t] * n_new, axis=0), c_bf))
    for s, c_bf in parts:
        online_update(s, c_bf)

    @pl.when(c == chunks_per_seq - 1)
    def _():
        cn = cn_ref[0].astype(BF16).astype(F32)
        kn = kn_ref[0].astype(BF16).astype(F32)
        snt = snt_ref[0]
        qlf = ql.astype(F32)
        qpf = qp.astype(F32)
        q_tok = lax.broadcasted_iota(jnp.int32, (rows, 1), 0) // n_heads
        m = m_ref[...]
        l = l_ref[...]
        acc = acc_ref[...]
        for tk in range(n_new):
            st = (jnp.sum(qlf * cn[tk:tk + 1, :], axis=-1, keepdims=True)
                  + jnp.sum(qpf * kn[tk:tk + 1, :], axis=-1, keepdims=True)) * snt[:, tk:tk + 1]
            vis = q_tok >= tk
            m_new = jnp.where(vis, jnp.maximum(m, st), m)
            p = jnp.where(vis, jnp.exp2(st - m_new), 0.0)
            alpha = jnp.exp2(m - m_new)
            l = alpha * l + p
            acc = alpha * acc + p.astype(BF16).astype(F32) * cn[tk:tk + 1, :]
            m = m_new
        o_lat = (acc / l).astype(BF16)
        full = jnp.dot(o_lat, wuv_ref[...], preferred_element_type=F32)
        r_head = lax.broadcasted_iota(jnp.int32, full.shape, 0) % n_heads
        c_head = lax.broadcasted_iota(jnp.int32, full.shape, 1) // v_head
        full = jnp.where(r_head == c_head, full, 0.0)
        o_ref[0] = jnp.sum(full.reshape(n_new, n_heads, full.shape[1]), axis=1).astype(BF16)


def _mla_sample(page_table, ql, qp, cn, kn, snt, w_uv_flat, ckv, kpe, ksc, n_new, n_heads, v_head):
    n_seq, rows = ql.shape[:2]
    page_size = ckv.shape[1]
    pages = min(PAGES_PER_CHUNK, page_table.shape[1])
    assert page_table.shape[1] % pages == 0 and pages % SAMPLE_SUBCHUNKS == 0
    per_seq = lambda a: pl.BlockSpec((1,) + a.shape[1:], lambda b, c, pt: (b, 0, 0))
    grid_spec = pltpu.PrefetchScalarGridSpec(
        num_scalar_prefetch=1,
        grid=(n_seq, page_table.shape[1] // pages),
        in_specs=[per_seq(ql), per_seq(qp), per_seq(cn), per_seq(kn), per_seq(snt),
                  pl.BlockSpec(w_uv_flat.shape, lambda b, c, pt: (0, 0)),
                  pl.BlockSpec(memory_space=pl.ANY), pl.BlockSpec(memory_space=pl.ANY),
                  pl.BlockSpec(memory_space=pl.ANY)],
        out_specs=pl.BlockSpec((1, n_new, n_heads * v_head), lambda b, c, pt: (b, 0, 0)),
        scratch_shapes=[pltpu.VMEM((2, pages, page_size, ckv.shape[2]), F32),
                        pltpu.VMEM((2, pages, kpe.shape[1], page_size), F32),
                        pltpu.VMEM((2, pages, ksc.shape[1], page_size), F32),
                        pltpu.SemaphoreType.DMA((3, 2)),
                        pltpu.VMEM((rows, 1), F32), pltpu.VMEM((rows, 1), F32),
                        pltpu.VMEM((rows, ckv.shape[2]), F32)])
    return pl.pallas_call(
        functools.partial(_mla_sample_kernel, n_new, n_heads, v_head),
        grid_spec=grid_spec,
        out_shape=jax.ShapeDtypeStruct((n_seq, n_new, n_heads * v_head), BF16),
        compiler_params=_params("arbitrary", "arbitrary"),
        name="mla_sample",
    )(page_table, ql, qp, cn, kn, snt, w_uv_flat, ckv, kpe, ksc)


def _mem_kv_kernel(mem_heads, mem_ref, g_ref, wk_ref, wv_ref, gk_ref, k_ref, v_ref):
    m = _rms(mem_ref[...], g_ref[...]).astype(BF16)
    k = jnp.dot(m, wk_ref[...], preferred_element_type=F32)
    d = k.shape[1] // mem_heads
    for hd in range(mem_heads):
        k_ref[:, hd * d:(hd + 1) * d] = _rms(k[:, hd * d:(hd + 1) * d], gk_ref[...])
    v_ref[...] = jnp.dot(m, wv_ref[...], preferred_element_type=F32)


def _mem_kv(mem, w, mem_heads):
    n, d = mem.shape
    ins = [mem, w["gmem"], w["w_mk"], w["w_mv"], w["gmk"]]
    return pl.pallas_call(
        functools.partial(_mem_kv_kernel, mem_heads),
        grid=(1,),
        in_specs=[_const_spec(a.shape) for a in ins],
        out_specs=[_whole_out_spec((n, d))] * 2,
        out_shape=[jax.ShapeDtypeStruct((n, d), F32)] * 2,
        compiler_params=_params("arbitrary"),
        name="mem_kv",
    )(*ins)


def _mem_attend_kernel(mem_heads, q_ref, k_ref, v_ref, o_ref):
    q = q_ref[0]
    k = k_ref[0].astype(BF16)
    v = v_ref[0].astype(BF16)
    d = q.shape[1] // mem_heads
    for hd in range(mem_heads):
        sl = slice(hd * d, (hd + 1) * d)
        s = _bdot_nt(q[:, sl], k[:, sl])
        p = jnp.exp(s - jnp.max(s, axis=-1, keepdims=True))
        p = p / jnp.sum(p, axis=-1, keepdims=True)
        o_ref[0, :, sl] = jnp.dot(p.astype(BF16), v[:, sl], preferred_element_type=F32).astype(BF16)


def _mem_attend(q, k, v, mem_heads, shared_kv):
    g, tq, d = q.shape
    n_mem = k.shape[1]
    kv_map = (lambda i: (0, 0, 0)) if shared_kv else (lambda i: (i, 0, 0))
    return pl.pallas_call(
        functools.partial(_mem_attend_kernel, mem_heads),
        grid=(g,),
        in_specs=[pl.BlockSpec((1, tq, d), lambda i: (i, 0, 0)),
                  pl.BlockSpec((1, n_mem, d), kv_map), pl.BlockSpec((1, n_mem, d), kv_map)],
        out_specs=pl.BlockSpec((1, tq, d), lambda i: (i, 0, 0)),
        out_shape=jax.ShapeDtypeStruct((g, tq, d), BF16),
        compiler_params=_params("parallel"),
        name="mem_attend_prompt" if shared_kv else "mem_attend_sample",
    )(q, k, v)


def _mem_attend_sample_kernel(q_ref, k_hbm, v_hbm, o_ref, kbuf, vbuf, sems):
    b = pl.program_id(0)
    slot = b % 2
    mem_heads, _, d = kbuf.shape[1:]

    def copies(seq, slot):
        out = []
        for hd in range(mem_heads):
            out.append(pltpu.make_async_copy(k_hbm.at[seq, :, hd, :], kbuf.at[slot, hd], sems.at[0, slot]))
            out.append(pltpu.make_async_copy(v_hbm.at[seq, :, hd, :], vbuf.at[slot, hd], sems.at[1, slot]))
        return out

    @pl.when(b == 0)
    def _():
        _start_all(copies(0, 0), 1)

    @pl.when(b + 1 < pl.num_programs(0))
    def _():
        _start_all(copies(b + 1, 1 - slot), 1)

    for cp in copies(b, slot):
        cp.wait()

    q = q_ref[0]
    scores = [_bdot_nt(q[:, hd * d:(hd + 1) * d], kbuf[slot, hd]) for hd in range(mem_heads)]
    for hd, s in enumerate(scores):
        p = jnp.exp(s - jnp.max(s, axis=-1, keepdims=True))
        p = p / jnp.sum(p, axis=-1, keepdims=True)
        o_ref[0, :, hd * d:(hd + 1) * d] = _bdot(p, vbuf[slot, hd]).astype(BF16)


def _mem_attend_sample(q, k, v):
    g, tq, d = q.shape
    n_mem, mem_heads, mem_dim = k.shape[1:]
    return pl.pallas_call(
        _mem_attend_sample_kernel,
        grid=(g,),
        in_specs=[pl.BlockSpec((1, tq, d), lambda i: (i, 0, 0)),
                  pl.BlockSpec(memory_space=pl.ANY), pl.BlockSpec(memory_space=pl.ANY)],
        out_specs=pl.BlockSpec((1, tq, d), lambda i: (i, 0, 0)),
        out_shape=jax.ShapeDtypeStruct((g, tq, d), BF16),
        scratch_shapes=[pltpu.VMEM((2, mem_heads, n_mem, mem_dim), F32),
                        pltpu.VMEM((2, mem_heads, n_mem, mem_dim), F32),
                        pltpu.SemaphoreType.DMA((2, 2))],
        compiler_params=_params("arbitrary"),
        name="mem_attend_sample",
    )(q, k, v)


def _merge_kernel(x_ref, a_ref, mla_ref, mem_ref, g1_ref, wg_ref, wo_ref, wmo_ref, wout_ref, h_ref):
    x = x_ref[...]
    d = x.shape[1]
    gates = jax.nn.sigmoid(jnp.dot(_rms(x, g1_ref[...]).astype(BF16), wg_ref[...], preferred_element_type=F32))
    br_b = jnp.dot(mla_ref[...], wo_ref[...], preferred_element_type=F32)
    br_c = jnp.dot(mem_ref[...], wmo_ref[...], preferred_element_type=F32)
    mix = gates[:, :d] * a_ref[...] + gates[:, d:2 * d] * br_b + gates[:, 2 * d:] * br_c
    h_ref[...] = x + jnp.dot(mix.astype(BF16), wout_ref[...], preferred_element_type=F32)


def _merge(x, a, mla_o, mem_o, w, w_o, tm):
    n, d = x.shape
    ins = [x, a, mla_o, mem_o, w["g1"], w["w_gate"], w_o, w["w_mo"], w["w_out"]]
    return pl.pallas_call(
        _merge_kernel,
        grid=(n // tm,),
        in_specs=[_row_spec(tm, d), _row_spec(tm, d), _row_spec(tm, mla_o.shape[1]), _row_spec(tm, d)]
        + [_const_spec(a_.shape) for a_ in ins[4:]],
        out_specs=_row_spec(tm, d),
        out_shape=jax.ShapeDtypeStruct((n, d), F32),
        compiler_params=_params("parallel"),
        name="merge",
    )(*ins)


FFN_HALO = 8
FFN_CHUNK = 64


def _ffn_prompt_kernel(conv_w, x_ref, g2_ref, wup_ref, cw_ref, cb_ref, wdown_ref, y_ref, tail_ref, ext_ref, act_ref):
    tm = x_ref.shape[0]
    d_ff = wdown_ref.shape[0]

    @pl.when(pl.program_id(0) == 0)
    def _():
        ext_ref[0:FFN_HALO, :] = jnp.zeros((FFN_HALO, ext_ref.shape[1]), F32)

    x = x_ref[...]
    ext_ref[FFN_HALO:, :] = jnp.dot(_rms(x, g2_ref[...]).astype(BF16), wup_ref[...], preferred_element_type=F32)
    first = FFN_HALO - (conv_w - 1)
    for r0 in range(0, tm, FFN_CHUNK):
        hc = jnp.broadcast_to(cb_ref[...], (FFN_CHUNK, ext_ref.shape[1]))
        for k in range(conv_w):
            hc = hc + cw_ref[k:k + 1, :] * ext_ref[r0 + first + k:r0 + first + k + FFN_CHUNK, :]
        act_ref[r0:r0 + FFN_CHUNK, :] = (jax.nn.silu(hc[:, :d_ff]) * hc[:, d_ff:]).astype(BF16)
    y_ref[...] = x + jnp.dot(act_ref[...], wdown_ref[...], preferred_element_type=F32)
    tail = ext_ref[tm:tm + FFN_HALO, :]
    tail_ref[...] = tail
    ext_ref[0:FFN_HALO, :] = tail


def _ffn_prompt(x, w, tm):
    n, d = x.shape
    d_up = w["w_up"].shape[1]
    conv_w = w["ffn_conv_w"].shape[0]
    ins = [x, w["g2"], w["w_up"], w["ffn_conv_w"], w["ffn_conv_b"], w["w_down"]]
    return pl.pallas_call(
        functools.partial(_ffn_prompt_kernel, conv_w),
        grid=(n // tm,),
        in_specs=[_row_spec(tm, d)] + [_const_spec(a.shape) for a in ins[1:]],
        out_specs=[_row_spec(tm, d), _whole_out_spec((FFN_HALO, d_up))],
        out_shape=[jax.ShapeDtypeStruct((n, d), F32), jax.ShapeDtypeStruct((FFN_HALO, d_up), F32)],
        scratch_shapes=[pltpu.VMEM((tm + FFN_HALO, d_up), F32), pltpu.VMEM((tm, d_up // 2), BF16)],
        compiler_params=_params("arbitrary"),
        name="ffn_prompt",
    )(*ins)


def _ffn_sample_kernel(conv_w, n_new, x_ref, hist_ref, g2_ref, wup_ref, cw_ref, cb_ref, wdown_ref, y_ref, up_ref):
    n_seq = hist_ref.shape[1]
    d_ff = wdown_ref.shape[0]
    x = x_ref[...]
    up_ref[...] = jnp.dot(_rms(x, g2_ref[...]).astype(BF16), wup_ref[...], preferred_element_type=F32)

    def ext(j):
        if j < conv_w - 1:
            return hist_ref[j]
        j -= conv_w - 1
        return up_ref[j * n_seq:(j + 1) * n_seq, :]

    for s in range(n_new):
        hc = jnp.broadcast_to(cb_ref[...], (n_seq, up_ref.shape[1]))
        for k in range(conv_w):
            hc = hc + cw_ref[k:k + 1, :] * ext(s + k)
        act = (jax.nn.silu(hc[:, :d_ff]) * hc[:, d_ff:]).astype(BF16)
        rows = slice(s * n_seq, (s + 1) * n_seq)
        y_ref[rows, :] = x[rows, :] + jnp.dot(act, wdown_ref[...], preferred_element_type=F32)


def _ffn_sample(x, hist_t, w, n_new):
    n, d = x.shape
    d_up = w["w_up"].shape[1]
    conv_w = w["ffn_conv_w"].shape[0]
    ins = [x, hist_t, w["g2"], w["w_up"], w["ffn_conv_w"], w["ffn_conv_b"], w["w_down"]]
    return pl.pallas_call(
        functools.partial(_ffn_sample_kernel, conv_w, n_new),
        grid=(1,),
        in_specs=[_const_spec(a.shape) for a in ins],
        out_specs=[_whole_out_spec((n, d)), _whole_out_spec((n, d_up))],
        out_shape=[jax.ShapeDtypeStruct((n, d), F32), jax.ShapeDtypeStruct((n, d_up), F32)],
        compiler_params=_params("arbitrary"),
        name="ffn_sample",
    )(*ins)


def _pad_heads(w3, width):
    k, h, d = w3.shape
    return jnp.concatenate([w3, jnp.zeros((k, h, width - d), w3.dtype)], axis=-1).reshape(k, h * width)


def _rot_half(w, half):
    return jnp.concatenate([-w[..., half:], w[..., :half]], axis=-1)


def _swap_half(g, half):
    return jnp.concatenate([g[..., half:], g[..., :half]], axis=-1)


def _lane_place(v, offset):
    k, d = v.shape
    return jnp.concatenate([jnp.zeros((k, offset), v.dtype), v, jnp.zeros((k, HEAD_PAD - offset - d), v.dtype)], axis=-1)


def _prep_weights(norm1_g, w_in, q_a_norm_g, w_uq, kv_a_norm_g, w_uk, w_uv, q_norm_g, k_norm_g, w_o_mla, conv_w, conv_b,
                  conv_ln_g, conv_ln_b, w_conv_out, mem_norm_g, w_mk, w_mv, mq_norm_g, mk_norm_g, w_mo, w_out, norm2_g,
                  w_up, ffn_conv_w, ffn_conv_b, w_down, rope, mem_heads):
    hp = HEAD_PAD
    d = w_in.shape[0]
    c_conv = conv_w.shape[1]
    q_lora = w_uq.shape[0]
    kv_lora, n_heads, nope = w_uk.shape
    v_head = w_uv.shape[2]
    half = rope // 2
    o_qa = 2 * c_conv
    o_kva = o_qa + q_lora
    o_mq = o_kva + kv_lora + rope
    o_gate = o_mq + w_mk.shape[1]
    w_pe = w_in[:, o_kva + kv_lora:o_mq]
    w_main = jnp.concatenate([w_in[:, :o_kva + kv_lora], _lane_place(w_pe, nope), _lane_place(_rot_half(w_pe, half), nope),
                              w_in[:, o_mq:o_gate]], axis=1).astype(BF16)
    uq3 = w_uq.reshape(q_lora, n_heads, nope + rope)
    uq_rot3 = jnp.concatenate([jnp.zeros((q_lora, n_heads, nope), F32), _rot_half(uq3[..., nope:], half)], axis=-1)
    row = lambda g: g.reshape(1, -1)
    sel = (jnp.arange(n_heads * hp)[:, None]
           == ((jnp.arange(n_heads * rope) // rope) * hp + nope + jnp.arange(n_heads * rope) % rope)[None, :])
    w_abs = jnp.transpose(w_uk, (1, 2, 0))
    w_abs = jnp.concatenate([w_abs, jnp.zeros((n_heads, hp - nope, kv_lora), F32)], axis=1)
    w_o3 = w_o_mla.reshape(n_heads, v_head, d)
    w_o_pad = jnp.concatenate([w_o3, jnp.zeros((n_heads, hp - v_head, d), F32)], axis=1).reshape(n_heads * hp, d)
    return {
        "g1": row(norm1_g), "w_main": w_main, "w_gate": w_in[:, o_gate:].astype(BF16),
        "gqa": row(q_a_norm_g), "w_uq_pad": _pad_heads(uq3, hp).astype(BF16), "w_uq_rot": _pad_heads(uq_rot3, hp).astype(BF16),
        "gq_pad": _lane_place(row(q_norm_g), 0),
        "gq_rot": _lane_place(_swap_half(row(q_norm_g)[:, nope:], half), nope),
        "gkv": row(kv_a_norm_g), "w_uk_pad": _pad_heads(w_uk, hp).astype(BF16), "w_uv_pad": _pad_heads(w_uv, hp).astype(BF16),
        "gkn_pad": _lane_place(row(k_norm_g)[:, :nope], 0),
        "gkp_pad": _lane_place(row(k_norm_g)[:, nope:], nope),
        "gkp_rot": _lane_place(_swap_half(row(k_norm_g)[:, nope:], half), nope),
        "gmq": row(mq_norm_g), "w_abs": w_abs.astype(BF16), "sel_pe": sel.astype(BF16),
        "w_uv_flat": w_uv.reshape(kv_lora, n_heads * v_head).astype(BF16),
        "w_o_pad": w_o_pad.astype(BF16), "w_o": w_o_mla.astype(BF16),
        "conv_w": conv_w, "conv_b": row(conv_b), "conv_ln_g": row(conv_ln_g), "conv_ln_b": row(conv_ln_b),
        "conv_w8": jnp.broadcast_to(conv_w[:, None, :], (conv_w.shape[0], SUBLANES, c_conv)),
        "conv_b8": jnp.broadcast_to(row(conv_b), (SUBLANES, c_conv)),
        "w_conv_out": w_conv_out.astype(BF16),
        "gmem": row(mem_norm_g), "w_mk": w_mk.astype(BF16), "w_mv": w_mv.astype(BF16), "gmk": row(mk_norm_g),
        "w_mo": w_mo.astype(BF16), "w_out": w_out.astype(BF16),
        "g2": row(norm2_g), "w_up": w_up.astype(BF16), "ffn_conv_w": ffn_conv_w, "ffn_conv_b": row(ffn_conv_b),
        "w_down": w_down.astype(BF16),
    }


def _rope_tables(pos, rope, nope):
    half = rope // 2
    inv_freq = ROPE_THETA ** (-jnp.arange(half, dtype=F32) / half)
    ang = pos.astype(F32)[:, None] * inv_freq[None, :]
    cos, sin = jnp.cos(ang), jnp.sin(ang)
    n = pos.shape[0]
    pad = jnp.zeros((n, HEAD_PAD - nope - rope), F32)
    cos_t = jnp.concatenate([jnp.ones((n, nope), F32), cos, cos, pad], axis=-1)
    sin_t = jnp.concatenate([jnp.zeros((n, nope), F32), sin, sin, pad], axis=-1)
    return cos_t, sin_t


def kernel(x_prompt, x_sample, mem_prompt, cache_ckv, cache_kpe, cache_kscale, page_table, state_conv, state_ffn, cache_mem_k, cache_mem_v, norm1_g, w_in, q_a_norm_g, w_uq, kv_a_norm_g, w_uk, w_uv, q_norm_g, k_norm_g, w_o_mla, conv_w, conv_b, conv_ln_g, conv_ln_b, w_conv_out, mem_norm_g, w_mk, w_mv, mq_norm_g, mk_norm_g, w_mo, w_out, norm2_g, w_up, ffn_conv_w, ffn_conv_b, w_down):
    depth = w_in.shape[0]
    assert depth == 1, "single trunk layer"
    b_p, s_p, d = x_prompt.shape
    assert b_p == 1, "one prompt sequence"
    n_seq, n_new, _ = x_sample.shape
    page_size = cache_ckv.shape[2]
    n_past = page_table.shape[1] * page_size
    kv_lora, n_heads, nope = w_uk.shape[1:]
    rope = cache_kpe.shape[-1]
    v_head = w_uv.shape[-1]
    qk_dim = nope + rope
    mem_heads, mem_dim = cache_mem_k.shape[-2:]
    n_mem = mem_prompt.shape[1]
    c_conv = conv_w.shape[-1]
    conv_width = conv_w.shape[1]
    ffn_width = ffn_conv_w.shape[1]
    d_up = w_up.shape[-1]
    hp = HEAD_PAD
    dims = (c_conv, w_uq.shape[1], kv_lora, n_heads, nope, rope, v_head, mem_heads, mem_dim)

    w = _prep_weights(norm1_g[0], w_in[0], q_a_norm_g[0], w_uq[0], kv_a_norm_g[0], w_uk[0], w_uv[0], q_norm_g[0],
                      k_norm_g[0], w_o_mla[0], conv_w[0], conv_b[0], conv_ln_g[0], conv_ln_b[0], w_conv_out[0],
                      mem_norm_g[0], w_mk[0], w_mv[0], mq_norm_g[0], mk_norm_g[0], w_mo[0], w_out[0], norm2_g[0],
                      w_up[0], ffn_conv_w[0], ffn_conv_b[0], w_down[0], rope, mem_heads)

    tm = min(ROW_TILE, s_p)
    xp = x_prompt.reshape(s_p, d)
    cos_p, sin_p = _rope_tables(jnp.arange(s_p), rope, nope)
    u_p, ckv_p, kpe_p, ksc_p, mq_p, q_p, k_p, v_p = _front_end(xp, cos_p, sin_p, w, dims, False, tm)
    a_p = _conv_prompt(u_p, w, tm)
    tq = min(ATTN_Q_TILE, s_p)
    mla_p = _mla_prompt(q_p, k_p, v_p, n_heads, v_head, tq, max(tq, min(ATTN_KV_TILE, s_p)))
    mk_p, mv_p = _mem_kv(mem_prompt.reshape(n_mem, d), w, mem_heads)
    mem_o_p = _mem_attend(mq_p.reshape(s_p // tm, tm, d), mk_p[None], mv_p[None], mem_heads, True).reshape(s_p, d)
    h_p = _merge(xp, a_p, mla_p, mem_o_p, w, w["w_o_pad"], tm)
    y_p, ffn_tail = _ffn_prompt(h_p, w, tm)

    n_s = n_seq * n_new
    xs = jnp.transpose(x_sample, (1, 0, 2)).reshape(n_s, d)
    pos_s = jnp.repeat(n_past + jnp.arange(n_new), n_seq)
    cos_s, sin_s = _rope_tables(pos_s, rope, nope)
    tms = min(ROW_TILE, n_s)
    u_s, ckv_s, kpe_s, ksc_s, mq_s, qlat_s, qpe_s, _ = _front_end(xs, cos_s, sin_s, w, dims, True, tms)

    def seq_major(a):
        return jnp.transpose(a.reshape(n_new, n_seq, -1), (1, 0, 2))

    ext_t = jnp.concatenate([jnp.transpose(state_conv[0], (1, 0, 2)), u_s.reshape(n_new, n_seq, c_conv)], axis=0)
    a_s = _conv_sample(ext_t, w, n_new, min(32, n_seq)).reshape(n_s, d)

    ckv_s_b = seq_major(ckv_s)
    kpe_s_b = seq_major(kpe_s)[..., nope:nope + rope]
    ksc_s_b = seq_major(ksc_s)[..., :n_heads]
    pad_rows = (-n_new) % 8
    pad_new = lambda a: jnp.pad(a, ((0, 0), (0, pad_rows), (0, 0)))
    snt = jnp.tile(jnp.transpose(ksc_s_b, (0, 2, 1)), (1, n_new, 1))
    snt = jnp.pad(snt, ((0, 0), (0, 0), (0, pad_rows)))
    ql = seq_major(qlat_s).reshape(n_seq, n_new * n_heads, kv_lora)
    qp = seq_major(qpe_s).reshape(n_seq, n_new * n_heads, rope)
    mla_s = _mla_sample(page_table, ql, qp, pad_new(ckv_s_b), pad_new(kpe_s_b), snt, w["w_uv_flat"],
                        cache_ckv.reshape(cache_ckv.shape[1:]),
                        jnp.swapaxes(cache_kpe.reshape(cache_kpe.shape[1:]), 1, 2),
                        jnp.swapaxes(cache_kscale.reshape(cache_kscale.shape[1:]), 1, 2), n_new, n_heads, v_head)
    mla_s_t = jnp.transpose(mla_s, (1, 0, 2)).reshape(n_s, n_heads * v_head)

    mem_o_s = _mem_attend_sample(seq_major(mq_s), cache_mem_k.reshape(cache_mem_k.shape[1:]),
                                 cache_mem_v.reshape(cache_mem_v.shape[1:]))
    mem_o_s_t = jnp.transpose(mem_o_s, (1, 0, 2)).reshape(n_s, d)
    h_s = _merge(xs, a_s, mla_s_t, mem_o_s_t, w, w["w_o"], tms)
    y_s, up_s = _ffn_sample(h_s, jnp.transpose(state_ffn[0], (1, 0, 2)), w, n_new)

    p_ckv = ckv_p.reshape(1, 1, s_p, kv_lora)
    p_kpe = kpe_p[:, nope:nope + rope].reshape(1, 1, s_p, rope)
    p_ksc = ksc_p[:, :n_heads].reshape(1, 1, s_p, n_heads)
    p_conv = u_p[s_p - (conv_width - 1):].reshape(1, 1, conv_width - 1, c_conv)
    p_ffn = ffn_tail[FFN_HALO - (ffn_width - 1):].reshape(1, 1, ffn_width - 1, d_up)
    p_mem_k = mk_p.reshape(1, 1, n_mem, mem_heads, mem_dim)
    p_mem_v = mv_p.reshape(1, 1, n_mem, mem_heads, mem_dim)
    s_conv = jnp.transpose(ext_t[n_new:], (1, 0, 2))[None]
    ffn_ext = jnp.concatenate([state_ffn[0], seq_major(up_s)], axis=1)
    s_ffn = ffn_ext[:, n_new:][None]
    return (y_p.reshape(1, s_p, d), seq_major(y_s), p_ckv, p_kpe, p_ksc, p_conv, p_ffn, p_mem_k, p_mem_v,
            ckv_s_b[None], kpe_s_b[None], ksc_s_b[None], s_conv, s_ffn)
```

```python
import functools

import jax
import jax.numpy as jnp
from jax import lax
from jax.experimental import pallas as pl
from jax.experimental.pallas import tpu as pltpu

F32 = jnp.float32
BF16 = jnp.bfloat16
EPS = 1e-6
ROPE_THETA = 10000.0
HEAD_PAD = 128
VMEM_LIMIT = 56 * 1024 * 1024
ROW_TILE = 512
ATTN_Q_TILE = 2048
ATTN_KV_TILE = 2048
ATTN_DIAG_STRIPS = 2
PAGES_PER_CHUNK = 128
SAMPLE_SUBCHUNKS = 16
NEG_INF = float("-inf")
LOG2E = 1.4426950408889634


def _const_spec(shape):
    nd = len(shape)
    return pl.BlockSpec(shape, lambda *_: (0,) * nd, pipeline_mode=pl.Buffered(1))


def _whole_out_spec(shape):
    nd = len(shape)
    return pl.BlockSpec(shape, lambda *_: (0,) * nd)


def _row_spec(tm, width):
    return pl.BlockSpec((tm, width), lambda i: (i, 0))


def _params(*sem):
    return pltpu.CompilerParams(dimension_semantics=sem, vmem_limit_bytes=VMEM_LIMIT)


def _bdot(a, b):
    return jnp.dot(a.astype(BF16), b.astype(BF16), preferred_element_type=F32)


def _bdot_nt(a, b):
    return lax.dot_general(a.astype(BF16), b.astype(BF16), (((1,), (1,)), ((), ())), preferred_element_type=F32)


def _start_all(copies, group):
    for n, cp in enumerate(copies):
        cp.start(priority=(n // group) % 2)


def _rms(x, g):
    return x * lax.rsqrt(jnp.mean(x * x, axis=-1, keepdims=True) + EPS) * g


def _front_end_kernel(dims, sample, x_ref, cos_ref, sin_ref, g1_ref, wmain_ref, gqa_ref, wuq_ref, wuqr_ref,
                      gq_ref, gqr_ref, gkv_ref, wuk_ref, gkn_ref, gkp_ref, gkpr_ref, gmq_ref, *rest):
    c_conv, q_lora, kv_lora, n_heads, nope, rope, v_head, mem_heads, mem_dim = dims
    qk_dim = nope + rope
    sm_scale = qk_dim ** -0.5 * LOG2E
    if sample:
        wabs_ref, sel_ref, u_ref, ckv_ref, kpe_ref, ksc_ref, mq_ref, qlat_ref, qpe_ref, qs_ref = rest
    else:
        wuv_ref, u_ref, ckv_ref, kpe_ref, ksc_ref, mq_ref, q_ref, k_ref, v_ref = rest
    hp = HEAD_PAD
    h = _rms(x_ref[...], g1_ref[...]).astype(BF16)
    z = jnp.dot(h, wmain_ref[...], preferred_element_type=F32)
    o_qa = 2 * c_conv
    o_c = o_qa + q_lora
    o_pe = o_c + kv_lora
    o_per = o_pe + hp
    o_mq = o_per + hp
    u_ref[...] = z[:, :c_conv] * jax.nn.sigmoid(z[:, c_conv:o_qa])

    cos_t = cos_ref[...]
    sin_t = sin_ref[...]
    qa = _rms(z[:, o_qa:o_c], gqa_ref[...]).astype(BF16)
    q = jnp.dot(qa, wuq_ref[...], preferred_element_type=F32)
    qr = jnp.dot(qa, wuqr_ref[...], preferred_element_type=F32)
    g_cos = gq_ref[...] * cos_t
    g_sin = gqr_ref[...] * sin_t
    q_out = qs_ref if sample else q_ref
    for hd in range(n_heads):
        sl = slice(hd * hp, (hd + 1) * hp)
        qh = q[:, sl]
        r = lax.rsqrt(jnp.sum(qh * qh, axis=-1, keepdims=True) / qk_dim + EPS)
        qn = r * (qh * g_cos + qr[:, sl] * g_sin) * sm_scale
        q_out[:, sl] = qn.astype(BF16)
        if sample:
            qlat_ref[:, hd * kv_lora:(hd + 1) * kv_lora] = _bdot(qn * gkn_ref[...], wabs_ref[hd]).astype(BF16)
    if sample:
        qpe_ref[...] = jnp.dot(qs_ref[...], sel_ref[...], preferred_element_type=F32).astype(BF16)

    c_kv = _rms(z[:, o_c:o_pe], gkv_ref[...])
    ckv_ref[...] = c_kv
    pe = z[:, o_pe:o_per]
    k_pe = pe * (gkp_ref[...] * cos_t) + z[:, o_per:o_mq] * (gkpr_ref[...] * sin_t)
    kpe_ref[...] = k_pe
    ss_pe = jnp.sum(pe * pe, axis=-1, keepdims=True)
    c_bf = c_kv.astype(BF16)
    k_nope = jnp.dot(c_bf, wuk_ref[...], preferred_element_type=F32)
    lane = lax.broadcasted_iota(jnp.int32, (x_ref.shape[0], hp), 1)
    ksc_all = jnp.zeros((x_ref.shape[0], hp), F32)
    for hd in range(n_heads):
        sl = slice(hd * hp, (hd + 1) * hp)
        kn = k_nope[:, sl]
        ksc = lax.rsqrt((jnp.sum(kn * kn, axis=-1, keepdims=True) + ss_pe) / qk_dim + EPS)
        ksc_all = jnp.where(lane == hd, ksc, ksc_all)
        if not sample:
            k_ref[:, sl] = ((kn * gkn_ref[...] + k_pe) * ksc).astype(BF16)
    ksc_ref[...] = ksc_all
    if not sample:
        v = jnp.dot(c_bf, wuv_ref[...], preferred_element_type=F32)
        head_lane = lax.broadcasted_iota(jnp.int32, v.shape, 1) % hp
        v_ref[...] = jnp.where(head_lane == v_head, 1.0, v).astype(BF16)

    for hd in range(mem_heads):
        sl = slice(hd * mem_dim, (hd + 1) * mem_dim)
        mq_ref[:, sl] = (_rms(z[:, o_mq + hd * mem_dim:o_mq + (hd + 1) * mem_dim], gmq_ref[...])
                         * (mem_dim ** -0.5)).astype(BF16)


def _front_end(x, cos_t, sin_t, w, dims, sample, tm):
    n, d = x.shape
    c_conv, q_lora, kv_lora, n_heads, nope, rope, v_head, mem_heads, mem_dim = dims
    hp = HEAD_PAD
    ins =[x, cos_t, sin_t, w["g1"], w["w_main"], w["gqa"], w["w_uq_pad"], w["w_uq_rot"], w["gq_pad"], w["gq_rot"],
           w["gkv"], w["w_uk_pad"], w["gkn_pad"], w["gkp_pad"], w["gkp_rot"], w["gmq"]]
    ins += [w["w_abs"], w["sel_pe"]] if sample else [w["w_uv_pad"]]
    in_specs = [_row_spec(tm, d), _row_spec(tm, hp), _row_spec(tm, hp)] + [_const_spec(a.shape) for a in ins[3:]]
    outs = [(c_conv, F32), (kv_lora, F32), (hp, F32), (hp, F32), (mem_heads * mem_dim, BF16)]
    if sample:
        outs += [(n_heads * kv_lora, BF16), (n_heads * rope, BF16), (n_heads * hp, BF16)]
    else:
        outs += [(n_heads * hp, BF16)] * 3
    return pl.pallas_call(
        functools.partial(_front_end_kernel, dims, sample),
        grid=(n // tm,),
        in_specs=in_specs,
        out_specs=[_row_spec(tm, wd) for wd, _ in outs],
        out_shape=[jax.ShapeDtypeStruct((n, wd), dt) for wd, dt in outs],
        compiler_params=_params("parallel"),
        name="front_end_sample" if sample else "front_end_prompt",
    )(*ins)


CONV_HALO = 32
CONV_CHUNK = 16
SUBLANES = 8


def _conv_prompt_kernel(conv_w, cur_ref, prev_ref, w_ref, b_ref, lng_ref, lnb_ref, wout_ref, a_ref, ext_ref, sh_ref,
                        act_ref):
    tm = cur_ref.shape[0]
    prev = prev_ref[...]
    ext_ref[0:CONV_HALO, :] = jnp.where(pl.program_id(0) == 0, jnp.zeros_like(prev), prev)
    ext_ref[CONV_HALO:, :] = cur_ref[...]
    span = sh_ref.shape[1]
    for phase in range(1, SUBLANES):
        sh_ref[phase - 1] = ext_ref[phase:phase + span, :]
    first = CONV_HALO - (conv_w - 1)
    groups = CONV_CHUNK // SUBLANES
    for r0 in range(0, tm, CONV_CHUNK):
        accs = [b_ref[...]] * groups
        for k in range(conv_w):
            phase = (first + k) % SUBLANES
            base = r0 + first + k - phase
            w8 = w_ref[k]
            for g in range(groups):
                lo = base + g * SUBLANES
                tap = ext_ref[lo:lo + SUBLANES, :] if phase == 0 else sh_ref[phase - 1, lo:lo + SUBLANES, :]
                accs[g] = accs[g] + w8 * tap
        acc = jnp.concatenate(accs, axis=0)
        mu = jnp.mean(acc, axis=-1, keepdims=True)
        cen = acc - mu
        var = jnp.mean(cen * cen, axis=-1, keepdims=True)
        y = cen * lax.rsqrt(var + EPS) * lng_ref[...] + lnb_ref[...]
        act_ref[r0:r0 + CONV_CHUNK, :] = jax.nn.silu(y).astype(BF16)
    a_ref[...] = jnp.dot(act_ref[...], wout_ref[...], preferred_element_type=F32)


def _conv_prompt(u, w, tm):
    n, c = u.shape
    conv_w = w["conv_w"].shape[0]
    ratio = tm // CONV_HALO
    return pl.pallas_call(
        functools.partial(_conv_prompt_kernel, conv_w),
        grid=(n // tm,),
        in_specs=[_row_spec(tm, c),
                  pl.BlockSpec((CONV_HALO, c), lambda i: (jnp.maximum(i * ratio - 1, 0), 0)),
                  _const_spec(w["conv_w8"].shape), _const_spec((SUBLANES, c)), _const_spec((1, c)), _const_spec((1, c)),
                  _const_spec(w["w_conv_out"].shape)],
        out_specs=_row_spec(tm, c),
        out_shape=jax.ShapeDtypeStruct((n, c), F32),
        scratch_shapes=[pltpu.VMEM((tm + CONV_HALO, c), F32),
                        pltpu.VMEM((SUBLANES - 1, tm + CONV_HALO - SUBLANES, c), F32),
                        pltpu.VMEM((tm, c), BF16)],
        compiler_params=_params("parallel"),
        name="conv_prompt",
    )(u, u, w["conv_w8"], w["conv_b8"], w["conv_ln_g"], w["conv_ln_b"], w["w_conv_out"])


def _conv_sample_kernel(conv_w, n_new, ext_ref, w_ref, b_ref, lng_ref, lnb_ref, wout_ref, a_ref):
    bb, c = ext_ref.shape[1], ext_ref.shape[2]
    for s in range(n_new):
        acc = jnp.broadcast_to(b_ref[...], (bb, c))
        for k in range(conv_w):
            acc = acc + w_ref[k:k + 1, :] * ext_ref[s + k]
        mu = jnp.mean(acc, axis=-1, keepdims=True)
        cen = acc - mu
        var = jnp.mean(cen * cen, axis=-1, keepdims=True)
        y = cen * lax.rsqrt(var + EPS) * lng_ref[...] + lnb_ref[...]
        a_ref[s] = jnp.dot(jax.nn.silu(y).astype(BF16), wout_ref[...], preferred_element_type=F32)


def _conv_sample(ext_t, w, n_new, bb):
    t, b, c = ext_t.shape
    conv_w = w["conv_w"].shape[0]
    return pl.pallas_call(
        functools.partial(_conv_sample_kernel, conv_w, n_new),
        grid=(b // bb,),
        in_specs=[pl.BlockSpec((t, bb, c), lambda i: (0, i, 0)),
                  _const_spec(w["conv_w"].shape), _const_spec((1, c)), _const_spec((1, c)), _const_spec((1, c)),
                  _const_spec(w["w_conv_out"].shape)],
        out_specs=pl.BlockSpec((n_new, bb, c), lambda i: (0, i, 0)),
        out_shape=jax.ShapeDtypeStruct((n_new, b, c), F32),
        compiler_params=_params("parallel"),
        name="conv_sample",
    )(ext_t, w["conv_w"], w["conv_b"], w["conv_ln_g"], w["conv_ln_b"], w["w_conv_out"])


def _mla_prompt_kernel(v_head, tk, q_ref, k_ref, v_ref, o_ref):
    tq = q_ref.shape[0]
    ratio = tk // tq
    i = pl.program_id(1)
    q = q_ref[...]

    def update(start, width, r0, carry, masked):
        m, acc = carry
        s = _bdot_nt(q[r0:], k_ref[pl.ds(start, width), :])
        if masked:
            row = lax.broadcasted_iota(jnp.int32, s.shape, 0)
            col = lax.broadcasted_iota(jnp.int32, s.shape, 1)
            s = jnp.where(col <= row, s, NEG_INF)
        m_new = jnp.maximum(m[r0:], jnp.max(s, axis=-1, keepdims=True))
        p = jnp.exp2(s - m_new).astype(BF16)
        acc_new = jnp.exp2(m[r0:] - m_new) * acc[r0:] + jnp.dot(p, v_ref[pl.ds(start, width), :],
                                                                preferred_element_type=F32)
        if r0:
            m_new = jnp.concatenate([m[:r0], m_new], axis=0)
            acc_new = jnp.concatenate([acc[:r0], acc_new], axis=0)
        return m_new, acc_new

    carry = (jnp.full((tq, 1), NEG_INF, F32), jnp.zeros((tq, v_ref.shape[1]), F32))
    n_wide = i // ratio
    carry = lax.fori_loop(0, n_wide, lambda j, c: update(pl.multiple_of(j * tk, tk), tk, 0, c, False), carry)
    for extra in range(ratio - 1):
        carry = lax.cond(i % ratio > extra,
                         lambda c: update(pl.multiple_of((n_wide * ratio + extra) * tq, tq), tq, 0, c, False),
                         lambda c: c, carry)
    strip = tq // ATTN_DIAG_STRIPS
    for d in range(ATTN_DIAG_STRIPS):
        carry = update(pl.multiple_of(i * tq + d * strip, strip), strip, d * strip, carry, True)
    acc = carry[1]
    o_ref[...] = (acc / acc[:, v_head:v_head + 1]).astype(BF16)


def _mla_prompt(q, k, v, n_heads, v_head, tq, tk):
    n = q.shape[0]
    hp = HEAD_PAD
    assert tk % tq == 0
    return pl.pallas_call(
        functools.partial(_mla_prompt_kernel, v_head, tk),
        grid=(n_heads, n // tq),
        in_specs=[pl.BlockSpec((tq, hp), lambda h, i: (i, h)),
                  pl.BlockSpec((n, hp), lambda h, i: (0, h)),
                  pl.BlockSpec((n, hp), lambda h, i: (0, h))],
        out_specs=pl.BlockSpec((tq, hp), lambda h, i: (i, h)),
        out_shape=jax.ShapeDtypeStruct((n, n_heads * hp), BF16),
        compiler_params=_params("parallel", "arbitrary"),
        name="mla_prompt",
    )(q, k, v)


def _mla_sample_kernel(n_new, n_heads, v_head, pt_ref, ql_ref, qp_ref, cn_ref, kn_ref, snt_ref, wuv_ref,
                       ckv_hbm, kpe_hbm, ksc_hbm, o_ref, cbuf, pbuf, sbuf, sems, m_ref, l_ref, acc_ref):
    n_seq, n_pages = pt_ref.shape
    pages = cbuf.shape[1]
    page_size = cbuf.shape[2]
    chunks_per_seq = n_pages // pages
    n_chunks = n_seq * chunks_per_seq
    rows = ql_ref.shape[1]
    b = pl.program_id(0)
    c = pl.program_id(1)
    t = b * chunks_per_seq + c
    slot = t % 2

    def copies(t, slot):
        b = t // chunks_per_seq
        c = t % chunks_per_seq
        out = []
        for r in range(pages):
            page = pt_ref[b, c * pages + r]
            out.append(pltpu.make_async_copy(ckv_hbm.at[page], cbuf.at[slot, r], sems.at[0, slot]))
            out.append(pltpu.make_async_copy(kpe_hbm.at[page], pbuf.at[slot, r], sems.at[1, slot]))
            out.append(pltpu.make_async_copy(ksc_hbm.at[page], sbuf.at[slot, r], sems.at[2, slot]))
        return out

    def online_update(s, v_bf):
        m = m_ref[...]
        m_new = jnp.maximum(m, jnp.max(s, axis=-1, keepdims=True))
        p = jnp.exp2(s - m_new)
        alpha = jnp.exp2(m - m_new)
        l_ref[...] = alpha * l_ref[...] + jnp.sum(p, axis=-1, keepdims=True)
        acc_ref[...] = alpha * acc_ref[...] + jnp.dot(p.astype(BF16), v_bf, preferred_element_type=F32)
        m_ref[...] = m_new

    @pl.when(t == 0)
    def _():
        _start_all(copies(0, 0), 3)

    @pl.when(t + 1 < n_chunks)
    def _():
        _start_all(copies(t + 1, 1 - slot), 3)

    for cp in copies(t, slot):
        cp.wait()

    @pl.when(c == 0)
    def _():
        m_ref[...] = jnp.full(m_ref.shape, NEG_INF, F32)
        l_ref[...] = jnp.zeros(l_ref.shape, F32)
        acc_ref[...] = jnp.zeros(acc_ref.shape, F32)

    ql = ql_ref[0]
    qp = qp_ref[0]
    sub = pages // SAMPLE_SUBCHUNKS
    parts = []
    for i in range(SAMPLE_SUBCHUNKS):
        pr = range(i * sub, (i + 1) * sub)
        c_bf = cbuf[slot, i * sub:(i + 1) * sub].reshape(sub * page_size, cbuf.shape[3]).astype(BF16)
        kp_t = jnp.concatenate([pbuf[slot, r] for r in pr], axis=1)
        ks_t = jnp.concatenate([sbuf[slot, r] for r in pr], axis=1)
        s = _bdot_nt(ql, c_bf) + _bdot(qp, kp_t)
        parts.append((s * jnp.concatenate([ks_t] * n_new, axis=0), c_bf))
    for s, c_bf in parts:
        online_update(s, c_bf)

    @pl.when(c == chunks_per_seq - 1)
    def _():
        cn = cn_ref[0].astype(BF16).astype(F32)
        kn = kn_ref[0].astype(BF16).astype(F32)
        snt = snt_ref[0]
        qlf = ql.astype(F32)
        qpf = qp.astype(F32)
        q_tok = lax.broadcasted_iota(jnp.int32, (rows, 1), 0) // n_heads
        m = m_ref[...]
        l = l_ref[...]
        acc = acc_ref[...]
        for tk in range(n_new):
            st = (jnp.sum(qlf * cn[tk:tk + 1, :], axis=-1, keepdims=True)
                  + jnp.sum(qpf * kn[tk:tk + 1, :], axis=-1, keepdims=True)) * snt[:, tk:tk + 1]
            vis = q_tok >= tk
            m_new = jnp.where(vis, jnp.maximum(m, st), m)
            p = jnp.where(vis, jnp.exp2(st - m_new), 0.0)
            alpha = jnp.exp2(m - m_new)
            l = alpha * l + p
            acc = alpha * acc + p.astype(BF16).astype(F32) * cn[tk:tk + 1, :]
            m = m_new
        o_lat = (acc / l).astype(BF16)
        full = jnp.dot(o_lat, wuv_ref[...], preferred_element_type=F32)
        r_head = lax.broadcasted_iota(jnp.int32, full.shape, 0) % n_heads
        c_head = lax.broadcasted_iota(jnp.int32, full.shape, 1) // v_head
        full = jnp.where(r_head == c_head, full, 0.0)
        o_ref[0] = jnp.sum(full.reshape(n_new, n_heads, full.shape[1]), axis=1).astype(BF16)


def _mla_sample(page_table, ql, qp, cn, kn, snt, w_uv_flat, ckv, kpe, ksc, n_new, n_heads, v_head):
    n_seq, rows = ql.shape[:2]
    page_size = ckv.shape[1]
    pages = min(PAGES_PER_CHUNK, page_table.shape[1])
    assert page_table.shape[1] % pages == 0 and pages % SAMPLE_SUBCHUNKS == 0
    per_seq = lambda a: pl.BlockSpec((1,) + a.shape[1:], lambda b, c, pt: (b, 0, 0))
    grid_spec = pltpu.PrefetchScalarGridSpec(
        num_scalar_prefetch=1,
        grid=(n_seq, page_table.shape[1] // pages),
        in_specs=[per_seq(ql), per_seq(qp), per_seq(cn), per_seq(kn), per_seq(snt),
                  pl.BlockSpec(w_uv_flat.shape, lambda b, c, pt: (0, 0)),
                  pl.BlockSpec(memory_space=pl.ANY), pl.BlockSpec(memory_space=pl.ANY),
                  pl.BlockSpec(memory_space=pl.ANY)],
        out_specs=pl.BlockSpec((1, n_new, n_heads * v_head), lambda b, c, pt: (b, 0, 0)),
        scratch_shapes=[pltpu.VMEM((2, pages, page_size, ckv.shape[2]), F32),
                        pltpu.VMEM((2, pages, kpe.shape[1], page_size), F32),
                        pltpu.VMEM((2, pages, ksc.shape[1], page_size), F32),
                        pltpu.SemaphoreType.DMA((3, 2)),
                        pltpu.VMEM((rows, 1), F32), pltpu.VMEM((rows, 1), F32),
                        pltpu.VMEM((rows, ckv.shape[2]), F32)])
    return pl.pallas_call(
        functools.partial(_mla_sample_kernel, n_new, n_heads, v_head),
        grid_spec=grid_spec,
        out_shape=jax.ShapeDtypeStruct((n_seq, n_new, n_heads * v_head), BF16),
        compiler_params=_params("arbitrary", "arbitrary"),
        name="mla_sample",
    )(page_table, ql, qp, cn, kn, snt, w_uv_flat, ckv, kpe, ksc)


def _mem_kv_kernel(mem_heads, mem_ref, g_ref, wk_ref, wv_ref, gk_ref, k_ref, v_ref):
    m = _rms(mem_ref[...], g_ref[...]).astype(BF16)
    k = jnp.dot(m, wk_ref[...], preferred_element_type=F32)
    d = k.shape[1] // mem_heads
    for hd in range(mem_heads):
        k_ref[:, hd * d:(hd + 1) * d] = _rms(k[:, hd * d:(hd + 1) * d], gk_ref[...])
    v_ref[...] = jnp.dot(m, wv_ref[...], preferred_element_type=F32)


def _mem_kv(mem, w, mem_heads):
    n, d = mem.shape
    ins = [mem, w["gmem"], w["w_mk"], w["w_mv"], w["gmk"]]
    return pl.pallas_call(
        functools.partial(_mem_kv_kernel, mem_heads),
        grid=(1,),
        in_specs=[_const_spec(a.shape) for a in ins],
        out_specs=[_whole_out_spec((n, d))] * 2,
        out_shape=[jax.ShapeDtypeStruct((n, d), F32)] * 2,
        compiler_params=_params("arbitrary"),
        name="mem_kv",
    )(*ins)


def _mem_attend_kernel(mem_heads, q_ref, k_ref, v_ref, o_ref):
    q = q_ref[0]
    k = k_ref[0].astype(BF16)
    v = v_ref[0].astype(BF16)
    d = q.shape[1] // mem_heads
    for hd in range(mem_heads):
        sl = slice(hd * d, (hd + 1) * d)
        s = _bdot_nt(q[:, sl], k[:, sl])
        p = jnp.exp(s - jnp.max(s, axis=-1, keepdims=True))
        p = p / jnp.sum(p, axis=-1, keepdims=True)
        o_ref[0, :, sl] = jnp.dot(p.astype(BF16), v[:, sl], preferred_element_type=F32).astype(BF16)


def _mem_attend(q, k, v, mem_heads, shared_kv):
    g, tq, d = q.shape
    n_mem = k.shape[1]
    kv_map = (lambda i: (0, 0, 0)) if shared_kv else (lambda i: (i, 0, 0))
    return pl.pallas_call(
        functools.partial(_mem_attend_kernel, mem_heads),
        grid=(g,),
        in_specs=[pl.BlockSpec((1, tq, d), lambda i: (i, 0, 0)),
                  pl.BlockSpec((1, n_mem, d), kv_map), pl.BlockSpec((1, n_mem, d), kv_map)],
        out_specs=pl.BlockSpec((1, tq, d), lambda i: (i, 0, 0)),
        out_shape=jax.ShapeDtypeStruct((g, tq, d), BF16),
        compiler_params=_params("parallel"),
        name="mem_attend_prompt" if shared_kv else "mem_attend_sample",
    )(q, k, v)


def _mem_attend_sample_kernel(q_ref, k_hbm, v_hbm, o_ref, kbuf, vbuf, sems):
    b = pl.program_id(0)
    slot = b % 2
    mem_heads, _, d = kbuf.shape[1:]

    def copies(seq, slot):
        out = []
        for hd in range(mem_heads):
            out.append(pltpu.make_async_copy(k_hbm.at[seq, :, hd, :], kbuf.at[slot, hd], sems.at[0, slot]))
            out.append(pltpu.make_async_copy(v_hbm.at[seq, :, hd, :], vbuf.at[slot, hd], sems.at[1, slot]))
        return out

    @pl.when(b == 0)
    def _():
        _start_all(copies(0, 0), 1)

    @pl.when(b + 1 < pl.num_programs(0))
    def _():
        _start_all(copies(b + 1, 1 - slot), 1)

    for cp in copies(b, slot):
        cp.wait()

    q = q_ref[0]
    scores = [_bdot_nt(q[:, hd * d:(hd + 1) * d], kbuf[slot, hd]) for hd in range(mem_heads)]
    for hd, s in enumerate(scores):
        p = jnp.exp(s - jnp.max(s, axis=-1, keepdims=True))
        p = p / jnp.sum(p, axis=-1, keepdims=True)
        o_ref[0, :, hd * d:(hd + 1) * d] = _bdot(p, vbuf[slot, hd]).astype(BF16)


def _mem_attend_sample(q, k, v):
    g, tq, d = q.shape
    n_mem, mem_heads, mem_dim = k.shape[1:]
    return pl.pallas_call(
        _mem_attend_sample_kernel,
        grid=(g,),
        in_specs=[pl.BlockSpec((1, tq, d), lambda i: (i, 0, 0)),
                  pl.BlockSpec(memory_space=pl.ANY), pl.BlockSpec(memory_space=pl.ANY)],
        out_specs=pl.BlockSpec((1, tq, d), lambda i: (i, 0, 0)),
        out_shape=jax.ShapeDtypeStruct((g, tq, d), BF16),
        scratch_shapes=[pltpu.VMEM((2, mem_heads, n_mem, mem_dim), F32),
                        pltpu.VMEM((2, mem_heads, n_mem, mem_dim), F32),
                        pltpu.SemaphoreType.DMA((2, 2))],
        compiler_params=_params("arbitrary"),
        name="mem_attend_sample",
    )(q, k, v)


def _merge_kernel(x_ref, a_ref, mla_ref, mem_ref, g1_ref, wg_ref, wo_ref, wmo_ref, wout_ref, h_ref):
    x = x_ref[...]
    d = x.shape[1]
    gates = jax.nn.sigmoid(jnp.dot(_rms(x, g1_ref[...]).astype(BF16), wg_ref[...], preferred_element_type=F32))
    br_b = jnp.dot(mla_ref[...], wo_ref[...], preferred_element_type=F32)
    br_c = jnp.dot(mem_ref[...], wmo_ref[...], preferred_element_type=F32)
    mix = gates[:, :d] * a_ref[...] + gates[:, d:2 * d] * br_b + gates[:, 2 * d:] * br_c
    h_ref[...] = x + jnp.dot(mix.astype(BF16), wout_ref[...], preferred_element_type=F32)


def _merge(x, a, mla_o, mem_o, w, w_o, tm):
    n, d = x.shape
    ins = [x, a, mla_o, mem_o, w["g1"], w["w_gate"], w_o, w["w_mo"], w["w_out"]]
    return pl.pallas_call(
        _merge_kernel,
        grid=(n // tm,),
        in_specs=[_row_spec(tm, d), _row_spec(tm, d), _row_spec(tm, mla_o.shape[1]), _row_spec(tm, d)]
        + [_const_spec(a_.shape) for a_ in ins[4:]],
        out_specs=_row_spec(tm, d),
        out_shape=jax.ShapeDtypeStruct((n, d), F32),
        compiler_params=_params("parallel"),
        name="merge",
    )(*ins)


FFN_HALO = 8
FFN_CHUNK = 64


def _ffn_prompt_kernel(conv_w, x_ref, g2_ref, wup_ref, cw_ref, cb_ref, wdown_ref, y_ref, tail_ref, ext_ref, act_ref):
    tm = x_ref.shape[0]
    d_ff = wdown_ref.shape[0]

    @pl.when(pl.program_id(0) == 0)
    def _():
        ext_ref[0:FFN_HALO, :] = jnp.zeros((FFN_HALO, ext_ref.shape[1]), F32)

    x = x_ref[...]
    ext_ref[FFN_HALO:, :] = jnp.dot(_rms(x, g2_ref[...]).astype(BF16), wup_ref[...], preferred_element_type=F32)
    first = FFN_HALO - (conv_w - 1)
    for r0 in range(0, tm, FFN_CHUNK):
        hc = jnp.broadcast_to(cb_ref[...], (FFN_CHUNK, ext_ref.shape[1]))
        for k in range(conv_w):
            hc = hc + cw_ref[k:k + 1, :] * ext_ref[r0 + first + k:r0 + first + k + FFN_CHUNK, :]
        act_ref[r0:r0 + FFN_CHUNK, :] = (jax.nn.silu(hc[:, :d_ff]) * hc[:, d_ff:]).astype(BF16)
    y_ref[...] = x + jnp.dot(act_ref[...], wdown_ref[...], preferred_element_type=F32)
    tail = ext_ref[tm:tm + FFN_HALO, :]
    tail_ref[...] = tail
    ext_ref[0:FFN_HALO, :] = tail


def _ffn_prompt(x, w, tm):
    n, d = x.shape
    d_up = w["w_up"].shape[1]
    conv_w = w["ffn_conv_w"].shape[0]
    ins = [x, w["g2"], w["w_up"], w["ffn_conv_w"], w["ffn_conv_b"], w["w_down"]]
    return pl.pallas_call(
        functools.partial(_ffn_prompt_kernel, conv_w),
        grid=(n // tm,),
        in_specs=[_row_spec(tm, d)] + [_const_spec(a.shape) for a in ins[1:]],
        out_specs=[_row_spec(tm, d), _whole_out_spec((FFN_HALO, d_up))],
        out_shape=[jax.ShapeDtypeStruct((n, d), F32), jax.ShapeDtypeStruct((FFN_HALO, d_up), F32)],
        scratch_shapes=[pltpu.VMEM((tm + FFN_HALO, d_up), F32), pltpu.VMEM((tm, d_up // 2), BF16)],
        compiler_params=_params("arbitrary"),
        name="ffn_prompt",
    )(*ins)


def _ffn_sample_kernel(conv_w, n_new, x_ref, hist_ref, g2_ref, wup_ref, cw_ref, cb_ref, wdown_ref, y_ref, up_ref):
    n_seq = hist_ref.shape[1]
    d_ff = wdown_ref.shape[0]
    x = x_ref[...]
    up_ref[...] = jnp.dot(_rms(x, g2_ref[...]).astype(BF16), wup_ref[...], preferred_element_type=F32)

    def ext(j):
        if j < conv_w - 1:
            return hist_ref[j]
        j -= conv_w - 1
        return up_ref[j * n_seq:(j + 1) * n_seq, :]

    for s in range(n_new):
        hc = jnp.broadcast_to(cb_ref[...], (n_seq, up_ref.shape[1]))
        for k in range(conv_w):
            hc = hc + cw_ref[k:k + 1, :] * ext(s + k)
        act = (jax.nn.silu(hc[:, :d_ff]) * hc[:, d_ff:]).astype(BF16)
        rows = slice(s * n_seq, (s + 1) * n_seq)
        y_ref[rows, :] = x[rows, :] + jnp.dot(act, wdown_ref[...], preferred_element_type=F32)


def _ffn_sample(x, hist_t, w, n_new):
    n, d = x.shape
    d_up = w["w_up"].shape[1]
    conv_w = w["ffn_conv_w"].shape[0]
    ins = [x, hist_t, w["g2"], w["w_up"], w["ffn_conv_w"], w["ffn_conv_b"], w["w_down"]]
    return pl.pallas_call(
        functools.partial(_ffn_sample_kernel, conv_w, n_new),
        grid=(1,),
        in_specs=[_const_spec(a.shape) for a in ins],
        out_specs=[_whole_out_spec((n, d)), _whole_out_spec((n, d_up))],
        out_shape=[jax.ShapeDtypeStruct((n, d), F32), jax.ShapeDtypeStruct((n, d_up), F32)],
        compiler_params=_params("arbitrary"),
        name="ffn_sample",
    )(*ins)


def _pad_heads(w3, width):
    k, h, d = w3.shape
    return jnp.concatenate([w3, jnp.zeros((k, h, width - d), w3.dtype)], axis=-1).reshape(k, h * width)


def _rot_half(w, half):
    return jnp.concatenate([-w[..., half:], w[..., :half]], axis=-1)


def _swap_half(g, half):
    return jnp.concatenate([g[..., half:], g[..., :half]], axis=-1)


def _lane_place(v, offset):
    k, d = v.shape
    return jnp.concatenate([jnp.zeros((k, offset), v.dtype), v, jnp.zeros((k, HEAD_PAD - offset - d), v.dtype)], axis=-1)


def _prep_weights(norm1_g, w_in, q_a_norm_g, w_uq, kv_a_norm_g, w_uk, w_uv, q_norm_g, k_norm_g, w_o_mla, conv_w, conv_b,
                  conv_ln_g, conv_ln_b, w_conv_out, mem_norm_g, w_mk, w_mv, mq_norm_g, mk_norm_g, w_mo, w_out, norm2_g,
                  w_up, ffn_conv_w, ffn_conv_b, w_down, rope, mem_heads):
    hp = HEAD_PAD
    d = w_in.shape[0]
    c_conv = conv_w.shape[1]
    q_lora = w_uq.shape[0]
    kv_lora, n_heads, nope = w_uk.shape
    v_head = w_uv.shape[2]
    half = rope // 2
    o_qa = 2 * c_conv
    o_kva = o_qa + q_lora
    o_mq = o_kva + kv_lora + rope
    o_gate = o_mq + w_mk.shape[1]
    w_pe = w_in[:, o_kva + kv_lora:o_mq]
    w_main = jnp.concatenate([w_in[:, :o_kva + kv_lora], _lane_place(w_pe, nope), _lane_place(_rot_half(w_pe, half), nope),
                              w_in[:, o_mq:o_gate]], axis=1).astype(BF16)
    uq3 = w_uq.reshape(q_lora, n_heads, nope + rope)
    uq_rot3 = jnp.concatenate([jnp.zeros((q_lora, n_heads, nope), F32), _rot_half(uq3[..., nope:], half)], axis=-1)
    row = lambda g: g.reshape(1, -1)
    sel = (jnp.arange(n_heads * hp)[:, None]
           == ((jnp.arange(n_heads * rope) // rope) * hp + nope + jnp.arange(n_heads * rope) % rope)[None, :])
    w_abs = jnp.transpose(w_uk, (1, 2, 0))
    w_abs = jnp.concatenate([w_abs, jnp.zeros((n_heads, hp - nope, kv_lora), F32)], axis=1)
    w_o3 = w_o_mla.reshape(n_heads, v_head, d)
    w_o_pad = jnp.concatenate([w_o3, jnp.zeros((n_heads, hp - v_head, d), F32)], axis=1).reshape(n_heads * hp, d)
    return {
        "g1": row(norm1_g), "w_main": w_main, "w_gate": w_in[:, o_gate:].astype(BF16),
        "gqa": row(q_a_norm_g), "w_uq_pad": _pad_heads(uq3, hp).astype(BF16), "w_uq_rot": _pad_heads(uq_rot3, hp).astype(BF16),
        "gq_pad": _lane_place(row(q_norm_g), 0),
        "gq_rot": _lane_place(_swap_half(row(q_norm_g)[:, nope:], half), nope),
        "gkv": row(kv_a_norm_g), "w_uk_pad": _pad_heads(w_uk, hp).astype(BF16), "w_uv_pad": _pad_heads(w_uv, hp).astype(BF16),
        "gkn_pad": _lane_place(row(k_norm_g)[:, :nope], 0),
        "gkp_pad": _lane_place(row(k_norm_g)[:, nope:], nope),
        "gkp_rot": _lane_place(_swap_half(row(k_norm_g)[:, nope:], half), nope),
        "gmq": row(mq_norm_g), "w_abs": w_abs.astype(BF16), "sel_pe": sel.astype(BF16),
        "w_uv_flat": w_uv.reshape(kv_lora, n_heads * v_head).astype(BF16),
        "w_o_pad": w_o_pad.astype(BF16), "w_o": w_o_mla.astype(BF16),
        "conv_w": conv_w, "conv_b": row(conv_b), "conv_ln_g": row(conv_ln_g), "conv_ln_b": row(conv_ln_b),
        "conv_w8": jnp.broadcast_to(conv_w[:, None, :], (conv_w.shape[0], SUBLANES, c_conv)),
        "conv_b8": jnp.broadcast_to(row(conv_b), (SUBLANES, c_conv)),
        "w_conv_out": w_conv_out.astype(BF16),
        "gmem": row(mem_norm_g), "w_mk": w_mk.astype(BF16), "w_mv": w_mv.astype(BF16), "gmk": row(mk_norm_g),
        "w_mo": w_mo.astype(BF16), "w_out": w_out.astype(BF16),
        "g2": row(norm2_g), "w_up": w_up.astype(BF16), "ffn_conv_w": ffn_conv_w, "ffn_conv_b": row(ffn_conv_b),
        "w_down": w_down.astype(BF16),
    }


def _rope_tables(pos, rope, nope):
    half = rope // 2
    inv_freq = ROPE_THETA ** (-jnp.arange(half, dtype=F32) / half)
    ang = pos.astype(F32)[:, None] * inv_freq[None, :]
    cos, sin = jnp.cos(ang), jnp.sin(ang)
    n = pos.shape[0]
    pad = jnp.zeros((n, HEAD_PAD - nope - rope), F32)
    cos_t = jnp.concatenate([jnp.ones((n, nope), F32), cos, cos, pad], axis=-1)
    sin_t = jnp.concatenate([jnp.zeros((n, nope), F32), sin, sin, pad], axis=-1)
    return cos_t, sin_t


def kernel(x_prompt, x_sample, mem_prompt, cache_ckv, cache_kpe, cache_kscale, page_table, state_conv, state_ffn, cache_mem_k, cache_mem_v, norm1_g, w_in, q_a_norm_g, w_uq, kv_a_norm_g, w_uk, w_uv, q_norm_g, k_norm_g, w_o_mla, conv_w, conv_b, conv_ln_g, conv_ln_b, w_conv_out, mem_norm_g, w_mk, w_mv, mq_norm_g, mk_norm_g, w_mo, w_out, norm2_g, w_up, ffn_conv_w, ffn_conv_b, w_down):
    depth = w_in.shape[0]
    assert depth == 1, "single trunk layer"
    b_p, s_p, d = x_prompt.shape
    assert b_p == 1, "one prompt sequence"
    n_seq, n_new, _ = x_sample.shape
    page_size = cache_ckv.shape[2]
    n_past = page_table.shape[1] * page_size
    kv_lora, n_heads, nope = w_uk.shape[1:]
    rope = cache_kpe.shape[-1]
    v_head = w_uv.shape[-1]
    qk_dim = nope + rope
    mem_heads, mem_dim = cache_mem_k.shape[-2:]
    n_mem = mem_prompt.shape[1]
    c_conv = conv_w.shape[-1]
    conv_width = conv_w.shape[1]
    ffn_width = ffn_conv_w.shape[1]
    d_up = w_up.shape[-1]
    hp = HEAD_PAD
    dims = (c_conv, w_uq.shape[1], kv_lora, n_heads, nope, rope, v_head, mem_heads, mem_dim)

    w = _prep_weights(norm1_g[0], w_in[0], q_a_norm_g[0], w_uq[0], kv_a_norm_g[0], w_uk[0], w_uv[0], q_norm_g[0],
                      k_norm_g[0], w_o_mla[0], conv_w[0], conv_b[0], conv_ln_g[0], conv_ln_b[0], w_conv_out[0],
                      mem_norm_g[0], w_mk[0], w_mv[0], mq_norm_g[0], mk_norm_g[0], w_mo[0], w_out[0], norm2_g[0],
                      w_up[0], ffn_conv_w[0], ffn_conv_b[0], w_down[0], rope, mem_heads)

    tm = min(ROW_TILE, s_p)
    xp = x_prompt.reshape(s_p, d)
    cos_p, sin_p = _rope_tables(jnp.arange(s_p), rope, nope)
    u_p, ckv_p, kpe_p, ksc_p, mq_p, q_p, k_p, v_p = _front_end(xp, cos_p, sin_p, w, dims, False, tm)
    a_p = _conv_prompt(u_p, w, tm)
    tq = min(ATTN_Q_TILE, s_p)
    mla_p = _mla_prompt(q_p, k_p, v_p, n_heads, v_head, tq, max(tq, min(ATTN_KV_TILE, s_p)))
    mk_p, mv_p = _mem_kv(mem_prompt.reshape(n_mem, d), w, mem_heads)
    mem_o_p = _mem_attend(mq_p.reshape(s_p // tm, tm, d), mk_p[None], mv_p[None], mem_heads, True).reshape(s_p, d)
    h_p = _merge(xp, a_p, mla_p, mem_o_p, w, w["w_o_pad"], tm)
    y_p, ffn_tail = _ffn_prompt(h_p, w, tm)

    n_s = n_seq * n_new
    xs = jnp.transpose(x_sample, (1, 0, 2)).reshape(n_s, d)
    pos_s = jnp.repeat(n_past + jnp.arange(n_new), n_seq)
    cos_s, sin_s = _rope_tables(pos_s, rope, nope)
    tms = min(ROW_TILE, n_s)
    u_s, ckv_s, kpe_s, ksc_s, mq_s, qlat_s, qpe_s, _ = _front_end(xs, cos_s, sin_s, w, dims, True, tms)

    def seq_major(a):
        return jnp.transpose(a.reshape(n_new, n_seq, -1), (1, 0, 2))

    ext_t = jnp.concatenate([jnp.transpose(state_conv[0], (1, 0, 2)), u_s.reshape(n_new, n_seq, c_conv)], axis=0)
    a_s = _conv_sample(ext_t, w, n_new, min(32, n_seq)).reshape(n_s, d)

    ckv_s_b = seq_major(ckv_s)
    kpe_s_b = seq_major(kpe_s)[..., nope:nope + rope]
    ksc_s_b = seq_major(ksc_s)[..., :n_heads]
    pad_rows = (-n_new) % 8
    pad_new = lambda a: jnp.pad(a, ((0, 0), (0, pad_rows), (0, 0)))
    snt = jnp.tile(jnp.transpose(ksc_s_b, (0, 2, 1)), (1, n_new, 1))
    snt = jnp.pad(snt, ((0, 0), (0, 0), (0, pad_rows)))
    ql = seq_major(qlat_s).reshape(n_seq, n_new * n_heads, kv_lora)
    qp = seq_major(qpe_s).reshape(n_seq, n_new * n_heads, rope)
    mla_s = _mla_sample(page_table, ql, qp, pad_new(ckv_s_b), pad_new(kpe_s_b), snt, w["w_uv_flat"],
                        cache_ckv.reshape(cache_ckv.shape[1:]),
                        jnp.swapaxes(cache_kpe.reshape(cache_kpe.shape[1:]), 1, 2),
                        jnp.swapaxes(cache_kscale.reshape(cache_kscale.shape[1:]), 1, 2), n_new, n_heads, v_head)
    mla_s_t = jnp.transpose(mla_s, (1, 0, 2)).reshape(n_s, n_heads * v_head)

    mem_o_s = _mem_attend_sample(seq_major(mq_s), cache_mem_k.reshape(cache_mem_k.shape[1:]),
                                 cache_mem_v.reshape(cache_mem_v.shape[1:]))
    mem_o_s_t = jnp.transpose(mem_o_s, (1, 0, 2)).reshape(n_s, d)
    h_s = _merge(xs, a_s, mla_s_t, mem_o_s_t, w, w["w_o"], tms)
    y_s, up_s = _ffn_sample(h_s, jnp.transpose(state_ffn[0], (1, 0, 2)), w, n_new)

    p_ckv = ckv_p.reshape(1, 1, s_p, kv_lora)
    p_kpe = kpe_p[:, nope:nope + rope].reshape(1, 1, s_p, rope)
    p_ksc = ksc_p[:, :n_heads].reshape(1, 1, s_p, n_heads)
    p_conv = u_p[s_p - (conv_width - 1):].reshape(1, 1, conv_width - 1, c_conv)
    p_ffn = ffn_tail[FFN_HALO - (ffn_width - 1):].reshape(1, 1, ffn_width - 1, d_up)
    p_mem_k = mk_p.reshape(1, 1, n_mem, mem_heads, mem_dim)
    p_mem_v = mv_p.reshape(1, 1, n_mem, mem_heads, mem_dim)
    s_conv = jnp.transpose(ext_t[n_new:], (1, 0, 2))[None]
    ffn_ext = jnp.concatenate([state_ffn[0], seq_major(up_s)], axis=1)
    s_ffn = ffn_ext[:, n_new:][None]
    return (y_p.reshape(1, s_p, d), seq_major(y_s), p_ckv, p_kpe, p_ksc, p_conv, p_ffn, p_mem_k, p_mem_v,
            ckv_s_b[None], kpe_s_b[None], ksc_s_b[None], s_conv, s_ffn)
```
